```python
import math
import jax, jax.numpy as jnp
from jax import lax
import numpy as np

D_MODEL = 1024
BATCH = 8
SEQ = 2048
DEPTH = 2
DEC_BATCH = 128
DEC_SEQ = 4
PAST_LEN = 16384
PAGE_SIZE = 128

N_MIXERS = 2
N_A = (DEPTH + 1) // 2
N_B = DEPTH // 2
W_LRU = 1536
N_BLK = 16
BLK_W = W_LRU // N_BLK
CONV_W = 4
LRU_C = 8.0
EXPAND = 128
H_B = D_MODEL // EXPAND
DK_B = EXPAND
DV_B = D_MODEL // H_B
INNER_B = H_B * DK_B
CHUNK = 64
EPS = 1e-6

kernel_name = "hawk_hgrn2_hybrid_step"


def _rmsnorm(x, gain):
    xf = x.astype(jnp.float32)
    y = xf * lax.rsqrt(jnp.mean(xf * xf, axis=-1, keepdims=True) + EPS)
    return (y * gain.astype(jnp.float32)).astype(x.dtype)


def _causal_conv(xb, w, b, buf):
    T = xb.shape[1]
    xp = jnp.concatenate([buf.astype(xb.dtype), xb], axis=1)
    out = b
    for k in range(CONV_W):
        out = out + w[k] * xp[:, k:k + T]
    return out, xp[:, -(CONV_W - 1):]


def _lru_combine(e1, e2):
    a1, b1 = e1
    a2, b2 = e2
    return a1 * a2, a2 * b1 + b2


def _rg_lru(xc, w_r, b_r, w_i, b_i, lam, h0):
    Bsz, T, W = xc.shape
    xf = xc.astype(jnp.float32)
    xh = xf.reshape(Bsz, T, N_BLK, BLK_W)
    r = jax.nn.sigmoid(jnp.einsum('btni,nij->btnj', xh, w_r.astype(jnp.float32)).reshape(Bsz, T, W) + b_r)
    ig = jax.nn.sigmoid(jnp.einsum('btni,nij->btnj', xh, w_i.astype(jnp.float32)).reshape(Bsz, T, W) + b_i)
    log_a = -LRU_C * jax.nn.softplus(-lam.astype(jnp.float32)) * r
    a = jnp.exp(log_a)
    bterm = jnp.sqrt(-jnp.expm1(2.0 * log_a)) * (ig * xf)
    bterm = bterm.at[:, 0].add(a[:, 0] * h0.astype(jnp.float32))
    _, h = lax.associative_scan(_lru_combine, (a, bterm), axis=1)
    return h, h[:, -1]


def _lru_layer(x, g_norm, w_in, conv_w, conv_b, w_r, b_r, w_i, b_i, lam, w_out, h0, buf):
    xn = _rmsnorm(x, g_norm)
    u = xn @ w_in
    xb, gate = u[..., :W_LRU], u[..., W_LRU:]
    xc, new_buf = _causal_conv(xb, conv_w, conv_b, buf)
    h, h_last = _rg_lru(xc, w_r, b_r, w_i, b_i, lam, h0)
    y = (h * jax.nn.silu(gate.astype(jnp.float32))).astype(x.dtype)
    return x + y @ w_out, h_last.astype(x.dtype), new_buf.astype(x.dtype)


def _hgrn2_chunked(q, k, v, log_g, S0):
    Bsz, T, H, DK = q.shape
    DV = v.shape[-1]
    C = math.gcd(T, CHUNK)
    n = T // C

    def to_chunks(z):
        return z.reshape(Bsz, n, C, H, z.shape[-1]).transpose(1, 0, 3, 2, 4)

    qs, ks, vs, gs = to_chunks(q), to_chunks(k), to_chunks(v), to_chunks(log_g)
    causal = jnp.tril(jnp.ones((C, C), dtype=bool))

    def step(S, inp):
        qc, kc, vc, gc = inp
        cum = jnp.cumsum(gc, axis=2)
        inter = jnp.einsum('bhtk,bhkv->bhtv', qc * jnp.exp(cum), S)
        diff = cum[:, :, :, None, :] - cum[:, :, None, :, :]
        decay = jnp.exp(jnp.where(causal[:, :, None], diff, -jnp.inf))
        attn = jnp.einsum('bhtk,bhsk,bhtsk->bhts', qc, kc, decay)
        intra = jnp.einsum('bhts,bhsv->bhtv', attn, vc)
        last = cum[:, :, -1:, :]
        S_new = jnp.exp(last[:, :, 0, :])[..., None] * S + jnp.einsum(
            'bhsk,bhsv->bhkv', kc * jnp.exp(last - cum), vc)
        return S_new, inter + intra

    S_T, o = lax.scan(step, S0.astype(jnp.float32), (qs, ks, vs, gs))
    o = o.transpose(1, 0, 3, 2, 4).reshape(Bsz, T, H, DV)
    return o, S_T


def _hgrn_layer(x, g_norm, w_in, lb, o_gain, w_out, S0):
    Bsz, T, _ = x.shape
    xn = _rmsnorm(x, g_norm)
    u = (xn @ w_in).astype(jnp.float32)
    q, f, iv, gate = jnp.split(u, 4, axis=-1)
    q = jax.nn.silu(q).reshape(Bsz, T, H_B, DK_B)
    lbh = lb.reshape(H_B, DK_B)
    log_g = jnp.logaddexp(jnp.log(lbh), jnp.log1p(-lbh) + jax.nn.log_sigmoid(f.reshape(Bsz, T, H_B, DK_B)))
    k = -jnp.expm1(log_g)
    v = iv.reshape(Bsz, T, H_B, DV_B)
    o, S_T = _hgrn2_chunked(q, k, v, log_g, S0)
    o = o * lax.rsqrt(jnp.mean(o * o, axis=-1, keepdims=True) + EPS) * o_gain.reshape(H_B, DV_B).astype(jnp.float32)
    y = (o.reshape(Bsz, T, INNER_B) * jax.nn.silu(gate)).astype(x.dtype)
    return x + y @ w_out, S_T.astype(x.dtype)


def _trunk(x, h0s, buf0s, S0s, norm_gain, a_w_in, a_conv_w, a_conv_b, a_w_r, a_b_r, a_w_i, a_b_i,
           a_lambda, a_w_out, b_w_in, b_lb_logits, b_o_gain, b_w_out, final_gain):
    sm = jax.nn.softmax(b_lb_logits.astype(jnp.float32), axis=0)
    lb_all = jnp.cumsum(sm, axis=0) - sm[0]
    hs, bufs, Ss = [], [], []
    for i in range(DEPTH):
        j = i // N_MIXERS
        if i % N_MIXERS == 0:
            x, h, bf = _lru_layer(x, norm_gain[i], a_w_in[j], a_conv_w[j], a_conv_b[j], a_w_r[j], a_b_r[j],
                                  a_w_i[j], a_b_i[j], a_lambda[j], a_w_out[j], h0s[j], buf0s[j])
            hs.append(h)
            bufs.append(bf)
        else:
            x, S = _hgrn_layer(x, norm_gain[i], b_w_in[j], lb_all[i], b_o_gain[j], b_w_out[j], S0s[j])
            Ss.append(S)
    return _rmsnorm(x, final_gain), jnp.stack(hs), jnp.stack(bufs), jnp.stack(Ss)


def setup_inputs(seed: int = 0) -> dict:
    key = jax.random.key(seed)
    ks = jax.random.split(key, 24)
    nrm = jax.random.normal
    f32 = jnp.float32
    a_sig = jax.random.uniform(ks[12], (N_A, W_LRU), f32, 0.9, 0.999)
    return {
        "x_prompt": nrm(ks[0], (BATCH, SEQ, D_MODEL), f32),
        "x_sample": nrm(ks[1], (DEC_BATCH, DEC_SEQ, D_MODEL), f32),
        "state_lru_h": 0.5 * nrm(ks[2], (N_A, DEC_BATCH, W_LRU), f32),
        "state_lru_conv": nrm(ks[3], (N_A, DEC_BATCH, CONV_W - 1, W_LRU), f32),
        "state_hgrn": 0.5 * nrm(ks[4], (N_B, DEC_BATCH, H_B, DK_B, DV_B), f32),
        "norm_gain": 1.0 + 0.1 * nrm(ks[5], (DEPTH, D_MODEL), f32),
        "a_w_in": nrm(ks[6], (N_A, D_MODEL, 2 * W_LRU), f32) * D_MODEL ** -0.5,
        "a_conv_w": nrm(ks[7], (N_A, CONV_W, W_LRU), f32) * CONV_W ** -0.5,
        "a_conv_b": 0.01 * nrm(ks[8], (N_A, W_LRU), f32),
        "a_w_r": nrm(ks[9], (N_A, N_BLK, BLK_W, BLK_W), f32) * BLK_W ** -0.5,
        "a_b_r": 0.01 * nrm(ks[10], (N_A, W_LRU), f32),
        "a_w_i": nrm(ks[11], (N_A, N_BLK, BLK_W, BLK_W), f32) * BLK_W ** -0.5,
        "a_b_i": 0.01 * nrm(ks[13], (N_A, W_LRU), f32),
        "a_lambda": jnp.log(a_sig) - jnp.log1p(-a_sig),
        "a_w_out": nrm(ks[14], (N_A, W_LRU, D_MODEL), f32) * W_LRU ** -0.5,
        "b_w_in": nrm(ks[15], (N_B, D_MODEL, 4 * INNER_B), f32) * D_MODEL ** -0.5,
        "b_lb_logits": nrm(ks[16], (DEPTH, INNER_B), f32),
        "b_o_gain": 1.0 + 0.1 * nrm(ks[17], (N_B, INNER_B), f32),
        "b_w_out": nrm(ks[18], (N_B, INNER_B, D_MODEL), f32) * INNER_B ** -0.5,
        "final_gain": 1.0 + 0.1 * nrm(ks[19], (D_MODEL,), f32),
    }


def reference(x_prompt, x_sample, state_lru_h, state_lru_conv, state_hgrn, norm_gain, a_w_in, a_conv_w,
              a_conv_b, a_w_r, a_b_r, a_w_i, a_b_i, a_lambda, a_w_out, b_w_in, b_lb_logits, b_o_gain,
              b_w_out, final_gain):
    dt = x_prompt.dtype
    h0p = jnp.zeros((N_A, BATCH, W_LRU), dt)
    buf0p = jnp.zeros((N_A, BATCH, CONV_W - 1, W_LRU), dt)
    S0p = jnp.zeros((N_B, BATCH, H_B, DK_B, DV_B), dt)
    y_prompt, hp, bp, Sp = _trunk(x_prompt, h0p, buf0p, S0p, norm_gain, a_w_in, a_conv_w, a_conv_b, a_w_r,
                                  a_b_r, a_w_i, a_b_i, a_lambda, a_w_out, b_w_in, b_lb_logits, b_o_gain,
                                  b_w_out, final_gain)
    y_sample, hs, bs, Ss = _trunk(x_sample, state_lru_h, state_lru_conv, state_hgrn, norm_gain, a_w_in,
                                  a_conv_w, a_conv_b, a_w_r, a_b_r, a_w_i, a_b_i, a_lambda, a_w_out, b_w_in,
                                  b_lb_logits, b_o_gain, b_w_out, final_gain)
    return (y_prompt, y_sample, hp, bp, Sp, hs, bs, Ss)
```

```python
import functools

import jax
import jax.numpy as jnp
from jax import lax
from jax.experimental import pallas as pl
from jax.experimental.pallas import tpu as pltpu

F32 = jnp.float32
BF16 = jnp.bfloat16

D_MODEL = 1024
W_LRU = 1536
N_BLK = 16
BLK_W = W_LRU // N_BLK
CONV_W = 4
LRU_C = 8.0
H_B = 8
DK_B = 128
DV_B = 128
INNER_B = H_B * DK_B
EPS = 1e-6

LANES = 128
GATE_SUPER = 384
N_SUPER = W_LRU // GATE_SUPER
CHUNK = 64
LOG_CHUNK = 6
VMEM_LIMIT = 56 * 1024 * 1024
LOG_FLOOR = -1e4


def _sigmoid_pair(x):
    t = jnp.exp(-jnp.abs(x))
    r = 1.0 / (1.0 + t)
    tr = t * r
    pos = x >= 0
    return jnp.where(pos, r, tr), jnp.where(pos, tr, r)


def _silu(x):
    s, _ = _sigmoid_pair(x)
    return x * s


def _rms_scale(x, gain):
    ms = jnp.mean(x * x, axis=-1, keepdims=True)
    return x * lax.rsqrt(ms + EPS) * gain


def _const_spec(shape):
    zeros = (0,) * len(shape)
    return pl.BlockSpec(shape, lambda *_: zeros, pipeline_mode=pl.Buffered(1))


def _lru_kernel(nb, tt, batch_major, x_ref, gain_ref, win_ref, cw_ref, cb_ref, wg_ref, br_ref, bi_ref,
                lam_ref, wout_ref, h0_ref, buf0_ref, x1_ref, hlast_ref, bufout_ref, *scratch):
    if batch_major:
        xs_s, os_s, xbuf_s, a_s, b_s, hc_s = scratch
    else:
        xbuf_s, a_s, b_s, hc_s = scratch
    rows = nb * tt
    halo = (CONV_W - 1) * nb
    n_lane_tiles = D_MODEL // LANES

    @pl.when(pl.program_id(0) == 0)
    def _():
        hc_s[...] = h0_ref[...]
        xbuf_s[0:halo, :] = buf0_ref[...]

    if batch_major:
        for b in range(nb):
            for j in range(n_lane_tiles):
                xs_s[j, pl.ds(b, tt, stride=nb), :] = x_ref[b, :, j * LANES:(j + 1) * LANES]
        x = jnp.concatenate([xs_s[j] for j in range(n_lane_tiles)], axis=-1)
    else:
        x = x_ref[...]

    xn = _rms_scale(x, gain_ref[...]).astype(BF16)
    u = jnp.dot(xn, win_ref[...], preferred_element_type=F32)
    xb = u[:, :W_LRU]
    gate = u[:, W_LRU:]

    xbuf_s[halo:halo + rows, :] = xb
    xc = cb_ref[...] + cw_ref[0:1, :] * xbuf_s[0:rows, :]
    for k in range(1, CONV_W):
        xc = xc + cw_ref[k:k + 1, :] * xbuf_s[k * nb:k * nb + rows, :]
    new_halo = xbuf_s[rows:rows + halo, :]
    bufout_ref[...] = new_halo
    xbuf_s[0:halo, :] = new_halo

    xcb = xc.astype(BF16)
    r_parts, i_parts = [], []
    for s in range(N_SUPER):
        gsi = jnp.dot(xcb[:, s * GATE_SUPER:(s + 1) * GATE_SUPER], wg_ref[s], preferred_element_type=F32)
        r_parts.append(gsi[:, :GATE_SUPER])
        i_parts.append(gsi[:, GATE_SUPER:])
    r, _ = _sigmoid_pair(jnp.concatenate(r_parts, axis=-1) + br_ref[...])
    ig, _ = _sigmoid_pair(jnp.concatenate(i_parts, axis=-1) + bi_ref[...])

    nlam = -lam_ref[...]
    softplus = jnp.maximum(nlam, 0.0) + jnp.log1p(jnp.exp(-jnp.abs(nlam)))
    log_a = (-LRU_C) * softplus * r
    a = jnp.exp(log_a)
    one_m_a2 = -jnp.tanh(log_a) * (a * a + 1.0)
    a_s[...] = a
    b_s[...] = jnp.sqrt(one_m_a2) * (ig * xc)

    def step(t, h):
        off = pl.multiple_of(t * nb, nb)
        h = a_s[pl.ds(off, nb), :] * h + b_s[pl.ds(off, nb), :]
        b_s[pl.ds(off, nb), :] = h
        return h

    h_end = lax.fori_loop(0, tt, step, hc_s[...])
    hc_s[...] = h_end
    hlast_ref[...] = h_end

    y = (b_s[...] * _silu(gate)).astype(BF16)
    out = x + jnp.dot(y, wout_ref[...], preferred_element_type=F32)

    if batch_major:
        for j in range(n_lane_tiles):
            os_s[j] = out[:, j * LANES:(j + 1) * LANES]
        for b in range(nb):
            x1_ref[b] = jnp.concatenate(
                [os_s[j, pl.ds(b, tt, stride=nb), :] for j in range(n_lane_tiles)], axis=-1)
    else:
        x1_ref[...] = out


def _lru_layer(x, h0, buf0, gain, w_in, conv_w, conv_b, wg, b_r, b_i, lam, w_out, *, nb, tt, batch_major):
    if batch_major:
        total_t = x.shape[1]
        x_spec = pl.BlockSpec((nb, tt, D_MODEL), lambda i: (0, i, 0))
        x_shape = jax.ShapeDtypeStruct((nb, total_t, D_MODEL), F32)
    else:
        total_t = x.shape[0] // nb
        x_spec = pl.BlockSpec((nb * tt, D_MODEL), lambda i: (i, 0))
        x_shape = jax.ShapeDtypeStruct((nb * total_t, D_MODEL), F32)
    rows = nb * tt
    halo = (CONV_W - 1) * nb
    scratch = []
    if batch_major:
        scratch += [pltpu.VMEM((D_MODEL // LANES, rows, LANES), F32),
                    pltpu.VMEM((D_MODEL // LANES, rows, LANES), F32)]
    scratch += [pltpu.VMEM((rows + halo, W_LRU), F32),
                pltpu.VMEM((rows, W_LRU), F32),
                pltpu.VMEM((rows, W_LRU), F32),
                pltpu.VMEM((nb, W_LRU), F32)]
    return pl.pallas_call(
        functools.partial(_lru_kernel, nb, tt, batch_major),
        grid=(total_t // tt,),
        in_specs=[x_spec,
                  _const_spec((1, D_MODEL)),
                  _const_spec((D_MODEL, 2 * W_LRU)),
                  _const_spec((CONV_W, W_LRU)),
                  _const_spec((1, W_LRU)),
                  _const_spec((N_SUPER, GATE_SUPER, 2 * GATE_SUPER)),
                  _const_spec((1, W_LRU)),
                  _const_spec((1, W_LRU)),
                  _const_spec((1, W_LRU)),
                  _const_spec((W_LRU, D_MODEL)),
                  _const_spec((nb, W_LRU)),
                  _const_spec((halo, W_LRU))],
        out_specs=[x_spec,
                   pl.BlockSpec((nb, W_LRU), lambda i: (0, 0)),
                   pl.BlockSpec((halo, W_LRU), lambda i: (0, 0))],
        out_shape=[x_shape,
                   jax.ShapeDtypeStruct((nb, W_LRU), F32),
                   jax.ShapeDtypeStruct((halo, W_LRU), F32)],
        scratch_shapes=scratch,
        compiler_params=pltpu.CompilerParams(dimension_semantics=("arbitrary",),
                                             vmem_limit_bytes=VMEM_LIMIT),
        name="lru_layer_bm" if batch_major else "lru_layer_tm",
    )(x, gain, w_in, conv_w, conv_b, wg, b_r, b_i, lam, w_out, h0, buf0)


def _lower_bound(lbl_ref):
    l0 = lbl_ref[0:1, :]
    l1 = lbl_ref[1:2, :]
    mx = jnp.maximum(l0, l1)
    e0 = jnp.exp(l0 - mx)
    e1 = jnp.exp(l1 - mx)
    return e1 / (e0 + e1)


def _hgrn_front(x, gain_ref, win_ref, lbl_ref):
    xn = _rms_scale(x, gain_ref[...]).astype(BF16)
    n = INNER_B
    q = _silu(jnp.dot(xn, win_ref[:, 0:n], preferred_element_type=F32))
    f = jnp.dot(xn, win_ref[:, n:2 * n], preferred_element_type=F32)
    v = jnp.dot(xn, win_ref[:, 2 * n:3 * n], preferred_element_type=F32)
    gs = _silu(jnp.dot(xn, win_ref[:, 3 * n:4 * n], preferred_element_type=F32))
    lb = _lower_bound(lbl_ref)
    sig, nsig = _sigmoid_pair(f)
    kk = (1.0 - lb) * nsig
    g = lb + (1.0 - lb) * sig
    lg = jnp.maximum(jnp.log(g), LOG_FLOOR)
    return q, kk, v, lg, gs


def _hgrn_back(x, o, gs, og_ref, wout_ref, fg_ref):
    parts = []
    for h in range(H_B):
        oh = o[:, h * DV_B:(h + 1) * DV_B]
        ms = jnp.mean(oh * oh, axis=-1, keepdims=True)
        parts.append(oh * lax.rsqrt(ms + EPS))
    on = jnp.concatenate(parts, axis=-1) * og_ref[...]
    y = (on * gs).astype(BF16)
    x2 = x + jnp.dot(y, wout_ref[...], preferred_element_type=F32)
    return _rms_scale(x2, fg_ref[...])


def _hgrn_prompt_kernel(tt, x_ref, gain_ref, win_ref, lbl_ref, og_ref, wout_ref, fg_ref,
                        y_ref, st_ref, q_s, k_s, v_s, lg_s, cum_s, o_s, st_s):
    i = pl.program_id(1)
    n_chunks = tt // CHUNK

    @pl.when(i == 0)
    def _():
        st_s[...] = jnp.zeros_like(st_s)

    cum_s[0:8, :] = jnp.zeros((8, INNER_B), F32)

    x = x_ref[...]
    q, kk, v, lg, gs = _hgrn_front(x, gain_ref, win_ref, lbl_ref)
    q_s[...] = q
    k_s[...] = kk
    v_s[...] = v.astype(BF16)
    lg_s[...] = lg

    row_c = lax.broadcasted_iota(jnp.int32, (CHUNK, CHUNK), 0)
    col_c = lax.broadcasted_iota(jnp.int32, (CHUNK, CHUNK), 1)
    tri = (row_c >= col_c).astype(BF16)
    xor_c = row_c ^ col_c
    row_l = lax.broadcasted_iota(jnp.int32, (CHUNK, LANES), 0)
    sub8 = lax.broadcasted_iota(jnp.int32, (8, LANES), 0)

    def chunk_body(c, carry):
        base = pl.multiple_of(c * CHUNK, CHUNK)
        lgc = lg_s[pl.ds(base, CHUNK), :]
        h1 = lgc.astype(BF16)
        r1 = lgc - h1.astype(F32)
        h2 = r1.astype(BF16)
        h3 = (r1 - h2.astype(F32)).astype(BF16)
        cum = (jnp.dot(tri, h1, preferred_element_type=F32)
               + jnp.dot(tri, h2, preferred_element_type=F32)
               + jnp.dot(tri, h3, preferred_element_type=F32))
        cum_s[8:8 + CHUNK, :] = cum

        for h in range(H_B):
            sl = slice(h * DK_B, (h + 1) * DK_B)
            cu = cum[:, sl]
            qh = q_s[pl.ds(base, CHUNK), sl]
            kh = k_s[pl.ds(base, CHUNK), sl]
            vh = v_s[pl.ds(base, CHUNK), sl]

            att = jnp.zeros((CHUNK, CHUNK), F32)
            for lm in range(LOG_CHUNK):
                m = 1 << lm
                if lm >= 2:
                    pieces = []
                    for blk in range(CHUNK // (2 * m)):
                        ref_row = 8 + blk * 2 * m + m - 1
                        pieces.append(jnp.broadcast_to(cum_s[ref_row:ref_row + 1, sl], (2 * m, LANES)))
                    cm = jnp.concatenate(pieces, axis=0)
                elif lm == 1:
                    pieces = []
                    for blk in range(CHUNK // 8):
                        lo = jnp.broadcast_to(cum_s[8 + blk * 8 + 1:8 + blk * 8 + 2, sl], (8, LANES))
                        hi = jnp.broadcast_to(cum_s[8 + blk * 8 + 5:8 + blk * 8 + 6, sl], (8, LANES))
                        pieces.append(jnp.where(sub8 < 4, lo, hi))
                    cm = jnp.concatenate(pieces, axis=0)
                else:
                    cm = jnp.where((row_l & 1) == 1, cum_s[7:7 + CHUNK, sl], cu)
                decay = jnp.exp(-jnp.abs(cu - cm))
                right = ((row_l >> lm) & 1) == 1
                z = (jnp.where(right, qh, kh) * decay).astype(BF16)
                pz = lax.dot_general(z, z, (((1,), (1,)), ((), ())), preferred_element_type=F32)
                pair = ((xor_c >> lm) == 1) & (((row_c >> lm) & 1) == 1)
                att = jnp.where(pair, pz, att)
            pd = lax.dot_general(qh.astype(BF16), kh.astype(BF16), (((1,), (1,)), ((), ())),
                                 preferred_element_type=F32)
            att = jnp.where(row_c == col_c, pd, att)

            clast = cum[CHUNK - 1:CHUNK, sl]
            qe = (qh * jnp.exp(cu)).astype(BF16)
            ke = (kh * jnp.exp(clast - cu)).astype(BF16)
            st = st_s[h]
            o = (lax.dot_general(qe, st.astype(BF16), (((1,), (1,)), ((), ())), preferred_element_type=F32)
                 + jnp.dot(att.astype(BF16), vh, preferred_element_type=F32))
            st_s[h] = jnp.exp(clast) * st + lax.dot_general(
                vh, ke, (((0,), (0,)), ((), ())), preferred_element_type=F32)
            o_s[pl.ds(base, CHUNK), sl] = o
        return carry

    lax.fori_loop(0, n_chunks, chunk_body, 0)

    y_ref[...] = _hgrn_back(x, o_s[...], gs, og_ref, wout_ref, fg_ref)

    @pl.when(i == pl.num_programs(1) - 1)
    def _():
        for h in range(H_B):
            st_ref[h] = st_s[h].T


def _hgrn_prompt(x1, gain, w_in, lbl, o_gain, w_out, f_gain, *, tt):
    nb, total_t, _ = x1.shape
    return pl.pallas_call(
        functools.partial(_hgrn_prompt_kernel, tt),
        grid=(nb, total_t // tt),
        in_specs=[pl.BlockSpec((None, tt, D_MODEL), lambda b, i: (b, i, 0)),
                  _const_spec((1, D_MODEL)),
                  _const_spec((D_MODEL, 4 * INNER_B)),
                  _const_spec((2, INNER_B)),
                  _const_spec((1, INNER_B)),
                  _const_spec((INNER_B, D_MODEL)),
                  _const_spec((1, D_MODEL))],
        out_specs=[pl.BlockSpec((None, tt, D_MODEL), lambda b, i: (b, i, 0)),
                   pl.BlockSpec((None, H_B, DK_B, DV_B), lambda b, i: (b, 0, 0, 0))],
        out_shape=[jax.ShapeDtypeStruct((nb, total_t, D_MODEL), F32),
                   jax.ShapeDtypeStruct((nb, H_B, DK_B, DV_B), F32)],
        scratch_shapes=[pltpu.VMEM((tt, INNER_B), F32),
                        pltpu.VMEM((tt, INNER_B), F32),
                        pltpu.VMEM((tt, INNER_B), BF16),
                        pltpu.VMEM((tt, INNER_B), F32),
                        pltpu.VMEM((8 + CHUNK, INNER_B), F32),
                        pltpu.VMEM((tt, INNER_B), F32),
                        pltpu.VMEM((H_B, DV_B, DK_B), F32)],
        compiler_params=pltpu.CompilerParams(dimension_semantics=("arbitrary", "arbitrary"),
                                             vmem_limit_bytes=VMEM_LIMIT),
        name="hgrn_prompt",
    )(x1, gain, w_in, lbl, o_gain, w_out, f_gain)


def _hgrn_decode_kernel(nb, tt, bstep, x_ref, gain_ref, win_ref, lbl_ref, og_ref, wout_ref, fg_ref, s0_ref,
                        y_ref, snew_ref, qe_s, ke_s, dd_s, v_s, oi_s, gs_s, x_s, o_s):
    j = pl.program_id(0)
    rows = nb * tt

    def to_batch_major(dst, val, t):
        for h in range(H_B):
            dst[h, pl.ds(t, nb, stride=tt), :] = val[:, h * LANES:(h + 1) * LANES]

    @pl.when(j == 0)
    def _():
        x = x_ref[...]
        q, kk, v, lg, gs = _hgrn_front(x, gain_ref, win_ref, lbl_ref)
        ones_blk = jnp.ones((DK_B, DK_B), BF16)

        def slab(arr, t):
            return arr[t * nb:(t + 1) * nb, :]

        cum = [slab(lg, 0)]
        for t in range(1, tt):
            cum.append(cum[-1] + slab(lg, t))
        dd_s[...] = jnp.zeros_like(dd_s)
        d = jnp.exp(cum[tt - 1])
        d_hi = d.astype(BF16).astype(F32)
        to_batch_major(dd_s, d_hi, 0)
        to_batch_major(dd_s, d - d_hi, 1)
        for t in range(tt):
            qt = slab(q, t)
            acc = jnp.zeros((nb, INNER_B), F32)
            for s in range(t + 1):
                prod = qt * slab(kk, s)
                if s < t:
                    prod = prod * jnp.exp(cum[t] - cum[s])
                pb = prod.astype(BF16)
                att = jnp.concatenate(
                    [jnp.dot(pb[:, h * DK_B:(h + 1) * DK_B], ones_blk, preferred_element_type=F32)
                     for h in range(H_B)], axis=-1)
                acc = acc + att * slab(v, s)
            to_batch_major(oi_s, acc, t)
            to_batch_major(qe_s, qt * jnp.exp(cum[t]), t)
            to_batch_major(ke_s, slab(kk, t) * jnp.exp(cum[tt - 1] - cum[t]), t)
            to_batch_major(v_s, slab(v, t), t)
            to_batch_major(gs_s, slab(gs, t), t)
            to_batch_major(x_s, slab(x, t), t)

    row1 = lax.broadcasted_iota(jnp.int32, (8, LANES), 0)
    lane2 = lax.broadcasted_iota(jnp.int32, (8, 2 * DV_B), 1)
    row2 = lax.broadcasted_iota(jnp.int32, (8, 2 * DV_B), 0)
    own = (row1 < tt, row1 >= tt)
    ones_rows = tuple(
        jnp.where((lane2 >= DV_B) & (row2 >= lo) & (row2 < lo + 2), 1.0, 0.0).astype(BF16) for lo in (tt, 0))

    def pair_body(pi, carry):
        off = pl.multiple_of((j * (bstep // 2) + pi) * 8, 8)
        for h in range(H_B):
            qe = qe_s[h, pl.ds(off, 8), :].astype(BF16)
            ke = ke_s[h, pl.ds(off, 8), :]
            dd = pltpu.roll(dd_s[h, pl.ds(off, 8), :], tt, axis=0)
            vv = v_s[h, pl.ds(off, 8), :]
            inter = []
            for e in range(2):
                s0 = s0_ref[2 * pi + e, h]
                inter.append(jnp.dot(qe, s0.astype(BF16), preferred_element_type=F32))
                lhs = jnp.where(own[e], ke, dd).astype(BF16)
                ve = jnp.where(own[e], vv, 0.0).astype(BF16)
                rhs = jnp.concatenate([ve, jnp.zeros((8, DV_B), BF16)], axis=-1) + ones_rows[e]
                upd = lax.dot_general(lhs, rhs, (((0,), (0,)), ((), ())),
                                      preferred_element_type=F32)
                snew_ref[2 * pi + e, h] = upd[:, DV_B:] * s0 + upd[:, :DV_B]
            o_s[pl.ds(off, 8), h * DV_B:(h + 1) * DV_B] = (
                jnp.where(own[0], inter[0], inter[1]) + oi_s[h, pl.ds(off, 8), :])
        return carry

    lax.fori_loop(0, bstep // 2, pair_body, 0)

    @pl.when(j == pl.num_programs(0) - 1)
    def _():
        xb = jnp.concatenate([x_s[h] for h in range(H_B)], axis=-1)
        gsb = jnp.concatenate([gs_s[h] for h in range(H_B)], axis=-1)
        y_ref[...] = _hgrn_back(xb, o_s[...], gsb, og_ref, wout_ref, fg_ref)


def _hgrn_decode(x1, s0, gain, w_in, lbl, o_gain, w_out, f_gain, *, nb, tt, bstep):
    rows = nb * tt
    slab = pltpu.VMEM((H_B, rows, LANES), F32)
    return pl.pallas_call(
        functools.partial(_hgrn_decode_kernel, nb, tt, bstep),
        grid=(nb // bstep,),
        in_specs=[_const_spec((rows, D_MODEL)),
                  _const_spec((1, D_MODEL)),
                  _const_spec((D_MODEL, 4 * INNER_B)),
                  _const_spec((2, INNER_B)),
                  _const_spec((1, INNER_B)),
                  _const_spec((INNER_B, D_MODEL)),
                  _const_spec((1, D_MODEL)),
                  pl.BlockSpec((bstep, H_B, DK_B, DV_B), lambda j: (j, 0, 0, 0))],
        out_specs=[pl.BlockSpec((rows, D_MODEL), lambda j: (0, 0)),
                   pl.BlockSpec((bstep, H_B, DK_B, DV_B), lambda j: (j, 0, 0, 0))],
        out_shape=[jax.ShapeDtypeStruct((rows, D_MODEL), F32),
                   jax.ShapeDtypeStruct((nb, H_B, DK_B, DV_B), F32)],
        scratch_shapes=[slab, slab, slab, slab, slab, slab, slab,
                        pltpu.VMEM((rows, INNER_B), F32)],
        compiler_params=pltpu.CompilerParams(dimension_semantics=("arbitrary",),
                                             vmem_limit_bytes=VMEM_LIMIT),
        name="hgrn_decode",
    )(x1, gain, w_in, lbl, o_gain, w_out, f_gain, s0)


def _gate_weights(w_r, w_i):
    per = GATE_SUPER // BLK_W

    def block_diag(w):
        w4 = w.reshape(N_SUPER, per, BLK_W, BLK_W)
        eye = jnp.eye(per, dtype=w.dtype)
        return jnp.einsum('saij,ac->saicj', w4, eye).reshape(N_SUPER, GATE_SUPER, GATE_SUPER)

    return jnp.concatenate([block_diag(w_r), block_diag(w_i)], axis=-1).astype(BF16)


def kernel(x_prompt, x_sample, state_lru_h, state_lru_conv, state_hgrn, norm_gain, a_w_in, a_conv_w, a_conv_b,
           a_w_r, a_b_r, a_w_i, a_b_i, a_lambda, a_w_out, b_w_in, b_lb_logits, b_o_gain, b_w_out, final_gain):
    assert norm_gain.shape[0] == 2 and a_w_in.shape[0] == 1 and b_w_in.shape[0] == 1
    pb, pt, _ = x_prompt.shape
    sb, st, _ = x_sample.shape
    halo_t = CONV_W - 1

    row = lambda p: p.reshape(1, -1)
    lru_params = (row(norm_gain[0]), a_w_in[0].astype(BF16), a_conv_w[0], row(a_conv_b[0]),
                  _gate_weights(a_w_r[0], a_w_i[0]), row(a_b_r[0]), row(a_b_i[0]), row(a_lambda[0]),
                  a_w_out[0].astype(BF16))
    hgrn_params = (row(norm_gain[1]), b_w_in[0].astype(BF16), b_lb_logits, row(b_o_gain[0]),
                   b_w_out[0].astype(BF16), row(final_gain))

    x1p, hp, bufp = _lru_layer(x_prompt, jnp.zeros((pb, W_LRU), F32), jnp.zeros((halo_t * pb, W_LRU), F32),
                               *lru_params, nb=pb, tt=32, batch_major=True)
    y_prompt, sp = _hgrn_prompt(x1p, *hgrn_params, tt=256)
    bufp = bufp.reshape(halo_t, pb, W_LRU).transpose(1, 0, 2)

    xs = x_sample.transpose(1, 0, 2).reshape(st * sb, D_MODEL)
    bufs0 = state_lru_conv[0].transpose(1, 0, 2).reshape(halo_t * sb, W_LRU)
    x1s, hs, bufs = _lru_layer(xs, state_lru_h[0], bufs0, *lru_params, nb=sb, tt=st, batch_major=False)
    ys, ss = _hgrn_decode(x1s, state_hgrn[0], *hgrn_params, nb=sb, tt=st, bstep=4)
    bufs = bufs.reshape(halo_t, sb, W_LRU).transpose(1, 0, 2)

    return (y_prompt, ys.reshape(sb, st, D_MODEL), hp[None], bufp[None], sp[None],
            hs[None], bufs[None], ss[None])
```

```python
import functools

import jax
import jax.numpy as jnp
from jax import lax
from jax.experimental import pallas as pl
from jax.experimental.pallas import tpu as pltpu

F32 = jnp.float32
BF16 = jnp.bfloat16

D_MODEL = 1024
W_LRU = 1536
N_BLK = 16
BLK_W = W_LRU // N_BLK
CONV_W = 4
LRU_C = 8.0
H_B = 8
DK_B = 128
DV_B = 128
INNER_B = H_B * DK_B
EPS = 1e-6

LANES = 128
GATE_SUPER = 384
N_SUPER = W_LRU // GATE_SUPER
CHUNK = 64
LOG_CHUNK = 6
VMEM_LIMIT = 56 * 1024 * 1024
MAX_LOCAL_EXPONENT = 75.0
LOG_FLOOR = -1e4


def _sigmoid_pair(x):
    th = 0.5 * jnp.tanh(0.5 * x)
    return 0.5 + th, 0.5 - th


def _silu(x):
    return x * (0.5 + 0.5 * jnp.tanh(0.5 * x))


def _rms_scale(x, gain):
    ms = jnp.mean(x * x, axis=-1, keepdims=True)
    return x * lax.rsqrt(ms + EPS) * gain


def _const_spec(shape):
    zeros = (0,) * len(shape)
    return pl.BlockSpec(shape, lambda *_: zeros, pipeline_mode=pl.Buffered(1))


def _lru_kernel(nb, tt, batch_major, x_ref, gain_ref, win_ref, cw_ref, cb_ref, wg_ref, br_ref, bi_ref,
                lam_ref, wout_ref, h0_ref, buf0_ref, x1_ref, hlast_ref, bufout_ref, *scratch):
    if batch_major:
        xs_s, os_s, xbuf_s, a_s, b_s, hc_s = scratch
    else:
        xbuf_s, a_s, b_s, hc_s = scratch
    rows = nb * tt
    halo = (CONV_W - 1) * nb
    n_lane_tiles = D_MODEL // LANES

    @pl.when(pl.program_id(0) == 0)
    def _():
        hc_s[...] = h0_ref[...]
        xbuf_s[0:halo, :] = buf0_ref[...]

    if batch_major:
        for b in range(nb):
            for j in range(n_lane_tiles):
                xs_s[j, pl.ds(b, tt, stride=nb), :] = x_ref[b, :, j * LANES:(j + 1) * LANES]
        x = jnp.concatenate([xs_s[j] for j in range(n_lane_tiles)], axis=-1)
    else:
        x = x_ref[...]

    xn = _rms_scale(x, gain_ref[...]).astype(BF16)
    u = jnp.dot(xn, win_ref[...], preferred_element_type=F32)
    xb = u[:, :W_LRU]
    gate = u[:, W_LRU:]

    xbuf_s[halo:halo + rows, :] = xb
    xc = cb_ref[...] + cw_ref[0:1, :] * xbuf_s[0:rows, :]
    for k in range(1, CONV_W):
        xc = xc + cw_ref[k:k + 1, :] * xbuf_s[k * nb:k * nb + rows, :]
    new_halo = xbuf_s[rows:rows + halo, :]
    bufout_ref[...] = new_halo
    xbuf_s[0:halo, :] = new_halo

    xcb = xc.astype(BF16)
    r_parts, i_parts = [], []
    for s in range(N_SUPER):
        gsi = jnp.dot(xcb[:, s * GATE_SUPER:(s + 1) * GATE_SUPER], wg_ref[s], preferred_element_type=F32)
        r_parts.append(gsi[:, :GATE_SUPER])
        i_parts.append(gsi[:, GATE_SUPER:])
    r, _ = _sigmoid_pair(jnp.concatenate(r_parts, axis=-1) + br_ref[...])
    ig, _ = _sigmoid_pair(jnp.concatenate(i_parts, axis=-1) + bi_ref[...])

    nlam = -lam_ref[...]
    softplus = jnp.maximum(nlam, 0.0) + jnp.log1p(jnp.exp(-jnp.abs(nlam)))
    log_a = (-LRU_C) * softplus * r
    a = jnp.exp(log_a)
    one_m_a2 = -jnp.tanh(log_a) * (a * a + 1.0)
    a_s[...] = a
    b_s[...] = jnp.sqrt(one_m_a2) * (ig * xc)

    def step(t, h):
        off = pl.multiple_of(t * nb, nb)
        h = a_s[pl.ds(off, nb), :] * h + b_s[pl.ds(off, nb), :]
        b_s[pl.ds(off, nb), :] = h
        return h

    h_end = lax.fori_loop(0, tt, step, hc_s[...])
    hc_s[...] = h_end
    hlast_ref[...] = h_end

    y = (b_s[...] * _silu(gate)).astype(BF16)
    out = x + jnp.dot(y, wout_ref[...], preferred_element_type=F32)

    if batch_major:
        for j in range(n_lane_tiles):
            os_s[j] = out[:, j * LANES:(j + 1) * LANES]
        for b in range(nb):
            x1_ref[b] = jnp.concatenate(
                [os_s[j, pl.ds(b, tt, stride=nb), :] for j in range(n_lane_tiles)], axis=-1)
    else:
        x1_ref[...] = out


def _lru_layer(x, h0, buf0, gain, w_in, conv_w, conv_b, wg, b_r, b_i, lam, w_out, *, nb, tt, batch_major):
    if batch_major:
        total_t = x.shape[1]
        x_spec = pl.BlockSpec((nb, tt, D_MODEL), lambda i: (0, i, 0))
        x_shape = jax.ShapeDtypeStruct((nb, total_t, D_MODEL), F32)
    else:
        total_t = x.shape[0] // nb
        x_spec = pl.BlockSpec((nb * tt, D_MODEL), lambda i: (i, 0))
        x_shape = jax.ShapeDtypeStruct((nb * total_t, D_MODEL), F32)
    rows = nb * tt
    halo = (CONV_W - 1) * nb
    scratch = []
    if batch_major:
        scratch += [pltpu.VMEM((D_MODEL // LANES, rows, LANES), F32),
                    pltpu.VMEM((D_MODEL // LANES, rows, LANES), F32)]
    scratch += [pltpu.VMEM((rows + halo, W_LRU), F32),
                pltpu.VMEM((rows, W_LRU), F32),
                pltpu.VMEM((rows, W_LRU), F32),
                pltpu.VMEM((nb, W_LRU), F32)]
    return pl.pallas_call(
        functools.partial(_lru_kernel, nb, tt, batch_major),
        grid=(total_t // tt,),
        in_specs=[x_spec,
                  _const_spec((1, D_MODEL)),
                  _const_spec((D_MODEL, 2 * W_LRU)),
                  _const_spec((CONV_W, W_LRU)),
                  _const_spec((1, W_LRU)),
                  _const_spec((N_SUPER, GATE_SUPER, 2 * GATE_SUPER)),
                  _const_spec((1, W_LRU)),
                  _const_spec((1, W_LRU)),
                  _const_spec((1, W_LRU)),
                  _const_spec((W_LRU, D_MODEL)),
                  _const_spec((nb, W_LRU)),
                  _const_spec((halo, W_LRU))],
        out_specs=[x_spec,
                   pl.BlockSpec((nb, W_LRU), lambda i: (0, 0)),
                   pl.BlockSpec((halo, W_LRU), lambda i: (0, 0))],
        out_shape=[x_shape,
                   jax.ShapeDtypeStruct((nb, W_LRU), F32),
                   jax.ShapeDtypeStruct((halo, W_LRU), F32)],
        scratch_shapes=scratch,
        compiler_params=pltpu.CompilerParams(dimension_semantics=("arbitrary",),
                                             vmem_limit_bytes=VMEM_LIMIT),
        name="lru_layer_bm" if batch_major else "lru_layer_tm",
    )(x, gain, w_in, conv_w, conv_b, wg, b_r, b_i, lam, w_out, h0, buf0)


def _lower_bound(lbl_ref):
    l0 = lbl_ref[0:1, :]
    l1 = lbl_ref[1:2, :]
    mx = jnp.maximum(l0, l1)
    e0 = jnp.exp(l0 - mx)
    e1 = jnp.exp(l1 - mx)
    return e1 / (e0 + e1)


def _hgrn_front(x, gain_ref, win_ref, lbl_ref):
    xn = _rms_scale(x, gain_ref[...]).astype(BF16)
    n = INNER_B
    q = _silu(jnp.dot(xn, win_ref[:, 0:n], preferred_element_type=F32))
    f = jnp.dot(xn, win_ref[:, n:2 * n], preferred_element_type=F32)
    v = jnp.dot(xn, win_ref[:, 2 * n:3 * n], preferred_element_type=F32)
    gs = _silu(jnp.dot(xn, win_ref[:, 3 * n:4 * n], preferred_element_type=F32))
    lb = _lower_bound(lbl_ref)
    sig, nsig = _sigmoid_pair(f)
    kk = (1.0 - lb) * nsig
    g = lb + (1.0 - lb) * sig
    lg = jnp.maximum(jnp.log(g), LOG_FLOOR)
    return q, kk, v, lg, gs


def _hgrn_back(x, o, gs, og_ref, wout_ref, fg_ref):
    parts = []
    for h in range(H_B):
        oh = o[:, h * DV_B:(h + 1) * DV_B]
        ms = jnp.mean(oh * oh, axis=-1, keepdims=True)
        parts.append(oh * lax.rsqrt(ms + EPS))
    on = jnp.concatenate(parts, axis=-1) * og_ref[...]
    y = (on * gs).astype(BF16)
    x2 = x + jnp.dot(y, wout_ref[...], preferred_element_type=F32)
    return _rms_scale(x2, fg_ref[...])


def _nt(a, b):
    return lax.dot_general(a, b, (((1,), (1,)), ((), ())), preferred_element_type=F32)


def _tn(a, b):
    return lax.dot_general(a, b, (((0,), (0,)), ((), ())), preferred_element_type=F32)


def _hgrn_prompt_kernel(tt, x_ref, gain_ref, win_ref, lbl_ref, og_ref, wout_ref, fg_ref,
                        y_ref, st_ref, q_s, k_s, v_s, cum_s, o_s, st_s):
    i = pl.program_id(1)
    n_chunks = tt // CHUNK
    half = CHUNK // 2

    @pl.when(i == 0)
    def _():
        st_s[...] = jnp.zeros_like(st_s)

    x = x_ref[...]
    q, kk, v, lg, gs = _hgrn_front(x, gain_ref, win_ref, lbl_ref)
    q_s[...] = q
    k_s[...] = kk
    v_s[...] = v.astype(BF16)

    row_c = lax.broadcasted_iota(jnp.int32, (CHUNK, CHUNK), 0)
    col_c = lax.broadcasted_iota(jnp.int32, (CHUNK, CHUNK), 1)
    tri = (row_c >= col_c).astype(BF16)

    h1 = lg.astype(BF16)
    r1 = lg - h1.astype(F32)
    h2 = r1.astype(BF16)
    h3 = (r1 - h2.astype(F32)).astype(BF16)
    worst = jnp.zeros((1, INNER_B), F32)
    for c in range(n_chunks):
        rs = slice(c * CHUNK, (c + 1) * CHUNK)
        cum = (jnp.dot(tri, h1[rs], preferred_element_type=F32)
               + jnp.dot(tri, h2[rs], preferred_element_type=F32)
               + jnp.dot(tri, h3[rs], preferred_element_type=F32))
        cum_s[rs, :] = cum
        mid = cum[half - 1:half, :]
        worst = jnp.maximum(worst, jnp.maximum(-mid, mid - cum[CHUNK - 1:CHUNK, :]))
    local_ok = jnp.max(worst) <= MAX_LOCAL_EXPONENT

    def head_update(h, base, att, qe_b, ke_b, d):
        sl = slice(h * DK_B, (h + 1) * DK_B)
        vh = v_s[pl.ds(base, CHUNK), sl]
        st = st_s[h]
        o = _nt(qe_b[:, sl], st.astype(BF16)) + jnp.dot(att.astype(BF16), vh, preferred_element_type=F32)
        st_s[h] = d[:, sl] * st + _tn(vh, ke_b[:, sl])
        o_s[pl.ds(base, CHUNK), sl] = o

    def chunk_common(base):
        cu = cum_s[pl.ds(base, CHUNK), :]
        tail = cum_s[pl.ds(base + CHUNK - 8, 8), :]
        clast = tail[7:8, :]
        qf = q_s[pl.ds(base, CHUNK), :]
        kf = k_s[pl.ds(base, CHUNK), :]
        qe = qf * jnp.exp(cu)
        ke_b = (kf * jnp.exp(clast - cu)).astype(BF16)
        return cu, qf, kf, qe, ke_b, jnp.exp(clast)

    def fast_chunk(c, carry):
        base = pl.multiple_of(c * CHUNK, CHUNK)
        cu, qf, kf, qe, ke_b, d = chunk_common(base)
        cmid = cum_s[pl.ds(base + half - 8, 8), :][7:8, :]
        right = lax.broadcasted_iota(jnp.int32, (CHUNK, INNER_B), 0) >= half
        da = cu - cmid
        za = jnp.where(right, qf, kf) * jnp.exp(-jnp.abs(da))
        kl_b = (kf * jnp.exp(jnp.where(right, -da, -cu))).astype(BF16)
        ql_b = jnp.where(right, za, qe).astype(BF16)
        za_b = za.astype(BF16)
        qe_b = qe.astype(BF16)
        cross = (row_c >= half) & (col_c < half)
        local = (row_c >= col_c) & ((row_c >= half) == (col_c >= half))
        for h in range(H_B):
            sl = slice(h * DK_B, (h + 1) * DK_B)
            pa = _nt(za_b[:, sl], za_b[:, sl])
            pb = _nt(ql_b[:, sl], kl_b[:, sl])
            att = jnp.where(cross, pa, jnp.where(local, pb, 0.0))
            head_update(h, base, att, qe_b, ke_b, d)
        return carry

    def slow_chunk(c, carry):
        base = pl.multiple_of(c * CHUNK, CHUNK)
        cu_all, qf, kf, qe, ke_b, d = chunk_common(base)
        qe_b = qe.astype(BF16)
        xor_c = row_c ^ col_c
        row_l = lax.broadcasted_iota(jnp.int32, (CHUNK, LANES), 0)
        sub8 = lax.broadcasted_iota(jnp.int32, (8, LANES), 0)
        for h in range(H_B):
            sl = slice(h * DK_B, (h + 1) * DK_B)
            cu = cu_all[:, sl]
            qh = qf[:, sl]
            kh = kf[:, sl]

            def ref_row(r):
                grp = cu[(r // 8) * 8:(r // 8) * 8 + 8, :]
                return grp[r % 8:r % 8 + 1, :]

            att = jnp.zeros((CHUNK, CHUNK), F32)
            for lm in range(LOG_CHUNK):
                m = 1 << lm
                if lm >= 2:
                    cm = jnp.concatenate(
                        [jnp.broadcast_to(ref_row(blk * 2 * m + m - 1), (2 * m, LANES))
                         for blk in range(CHUNK // (2 * m))], axis=0)
                elif lm == 1:
                    cm = jnp.concatenate(
                        [jnp.where(sub8 < 4,
                                   jnp.broadcast_to(ref_row(blk * 8 + 1), (8, LANES)),
                                   jnp.broadcast_to(ref_row(blk * 8 + 5), (8, LANES)))
                         for blk in range(CHUNK // 8)], axis=0)
                else:
                    cm = jnp.where((row_l & 1) == 1, pltpu.roll(cu, 1, axis=0), cu)
                right = ((row_l >> lm) & 1) == 1
                z = (jnp.where(right, qh, kh) * jnp.exp(-jnp.abs(cu - cm))).astype(BF16)
                pair = ((xor_c >> lm) == 1) & (((row_c >> lm) & 1) == 1)
                att = jnp.where(pair, _nt(z, z), att)
            att = jnp.where(row_c == col_c, _nt(qh.astype(BF16), kh.astype(BF16)), att)
            head_update(h, base, att, qe_b, ke_b, d)
        return carry

    lax.cond(local_ok,
             lambda: lax.fori_loop(0, n_chunks, fast_chunk, 0),
             lambda: lax.fori_loop(0, n_chunks, slow_chunk, 0))

    y_ref[...] = _hgrn_back(x, o_s[...], gs, og_ref, wout_ref, fg_ref)

    @pl.when(i == pl.num_programs(1) - 1)
    def _():
        for h in range(H_B):
            st_ref[h] = st_s[h].T


def _hgrn_prompt(x1, gain, w_in, lbl, o_gain, w_out, f_gain, *, tt):
    nb, total_t, _ = x1.shape
    return pl.pallas_call(
        functools.partial(_hgrn_prompt_kernel, tt),
        grid=(nb, total_t // tt),
        in_specs=[pl.BlockSpec((None, tt, D_MODEL), lambda b, i: (b, i, 0)),
                  _const_spec((1, D_MODEL)),
                  _const_spec((D_MODEL, 4 * INNER_B)),
                  _const_spec((2, INNER_B)),
                  _const_spec((1, INNER_B)),
                  _const_spec((INNER_B, D_MODEL)),
                  _const_spec((1, D_MODEL))],
        out_specs=[pl.BlockSpec((None, tt, D_MODEL), lambda b, i: (b, i, 0)),
                   pl.BlockSpec((None, H_B, DK_B, DV_B), lambda b, i: (b, 0, 0, 0))],
        out_shape=[jax.ShapeDtypeStruct((nb, total_t, D_MODEL), F32),
                   jax.ShapeDtypeStruct((nb, H_B, DK_B, DV_B), F32)],
        scratch_shapes=[pltpu.VMEM((tt, INNER_B), F32),
                        pltpu.VMEM((tt, INNER_B), F32),
                        pltpu.VMEM((tt, INNER_B), BF16),
                        pltpu.VMEM((tt, INNER_B), F32),
                        pltpu.VMEM((tt, INNER_B), F32),
                        pltpu.VMEM((H_B, DV_B, DK_B), F32)],
        compiler_params=pltpu.CompilerParams(dimension_semantics=("arbitrary", "arbitrary"),
                                             vmem_limit_bytes=VMEM_LIMIT),
        name="hgrn_prompt",
    )(x1, gain, w_in, lbl, o_gain, w_out, f_gain)


def _hgrn_decode_kernel(nb, tt, bstep, x_ref, gain_ref, win_ref, lbl_ref, og_ref, wout_ref, fg_ref, s0_ref,
                        y_ref, snew_ref, qe_s, ke_s, dd_s, v_s, oi_s, gs_s, x_s, o_s):
    j = pl.program_id(0)
    rows = nb * tt

    def to_batch_major(dst, val, t):
        for h in range(H_B):
            dst[h, pl.ds(t, nb, stride=tt), :] = val[:, h * LANES:(h + 1) * LANES]

    @pl.when(j == 0)
    def _():
        x = x_ref[...]
        q, kk, v, lg, gs = _hgrn_front(x, gain_ref, win_ref, lbl_ref)
        ones_blk = jnp.ones((DK_B, DK_B), BF16)

        def slab(arr, t):
            return arr[t * nb:(t + 1) * nb, :]

        cum = [slab(lg, 0)]
        for t in range(1, tt):
            cum.append(cum[-1] + slab(lg, t))
        dd_s[...] = jnp.zeros_like(dd_s)
        d = jnp.exp(cum[tt - 1])
        d_hi = d.astype(BF16).astype(F32)
        to_batch_major(dd_s, d_hi, 0)
        to_batch_major(dd_s, d - d_hi, 1)
        for t in range(tt):
            qt = slab(q, t)
            acc = jnp.zeros((nb, INNER_B), F32)
            for s in range(t + 1):
                prod = qt * slab(kk, s)
                if s < t:
                    prod = prod * jnp.exp(cum[t] - cum[s])
                pb = prod.astype(BF16)
                att = jnp.concatenate(
                    [jnp.dot(pb[:, h * DK_B:(h + 1) * DK_B], ones_blk, preferred_element_type=F32)
                     for h in range(H_B)], axis=-1)
                acc = acc + att * slab(v, s)
            to_batch_major(oi_s, acc, t)
            to_batch_major(qe_s, qt * jnp.exp(cum[t]), t)
            to_batch_major(ke_s, slab(kk, t) * jnp.exp(cum[tt - 1] - cum[t]), t)
            to_batch_major(v_s, slab(v, t), t)
            to_batch_major(gs_s, slab(gs, t), t)
            to_batch_major(x_s, slab(x, t), t)

    row1 = lax.broadcasted_iota(jnp.int32, (8, LANES), 0)
    lane2 = lax.broadcasted_iota(jnp.int32, (8, 2 * DV_B), 1)
    row2 = lax.broadcasted_iota(jnp.int32, (8, 2 * DV_B), 0)
    own = (row1 < tt, row1 >= tt)
    ones_rows = tuple(
        jnp.where((lane2 >= DV_B) & (row2 >= lo) & (row2 < lo + 2), 1.0, 0.0).astype(BF16) for lo in (tt, 0))

    def pair_body(pi, carry):
        off = pl.multiple_of((j * (bstep // 2) + pi) * 8, 8)
        for h in range(H_B):
            qe = qe_s[h, pl.ds(off, 8), :].astype(BF16)
            ke = ke_s[h, pl.ds(off, 8), :]
            dd = pltpu.roll(dd_s[h, pl.ds(off, 8), :], tt, axis=0)
            vv = v_s[h, pl.ds(off, 8), :]
            inter = []
            for e in range(2):
                s0 = s0_ref[2 * pi + e, h]
                inter.append(jnp.dot(qe, s0.astype(BF16), preferred_element_type=F32))
                lhs = jnp.where(own[e], ke, dd).astype(BF16)
                ve = jnp.where(own[e], vv, 0.0).astype(BF16)
                rhs = jnp.concatenate([ve, jnp.zeros((8, DV_B), BF16)], axis=-1) + ones_rows[e]
                upd = lax.dot_general(lhs, rhs, (((0,), (0,)), ((), ())),
                                      preferred_element_type=F32)
                snew_ref[2 * pi + e, h] = upd[:, DV_B:] * s0 + upd[:, :DV_B]
            o_s[pl.ds(off, 8), h * DV_B:(h + 1) * DV_B] = (
                jnp.where(own[0], inter[0], inter[1]) + oi_s[h, pl.ds(off, 8), :])
        return carry

    lax.fori_loop(0, bstep // 2, pair_body, 0)

    @pl.when(j == pl.num_programs(0) - 1)
    def _():
        xb = jnp.concatenate([x_s[h] for h in range(H_B)], axis=-1)
        gsb = jnp.concatenate([gs_s[h] for h in range(H_B)], axis=-1)
        y_ref[...] = _hgrn_back(xb, o_s[...], gsb, og_ref, wout_ref, fg_ref)


def _hgrn_decode(x1, s0, gain, w_in, lbl, o_gain, w_out, f_gain, *, nb, tt, bstep):
    rows = nb * tt
    slab = pltpu.VMEM((H_B, rows, LANES), F32)
    return pl.pallas_call(
        functools.partial(_hgrn_decode_kernel, nb, tt, bstep),
        grid=(nb // bstep,),
        in_specs=[_const_spec((rows, D_MODEL)),
                  _const_spec((1, D_MODEL)),
                  _const_spec((D_MODEL, 4 * INNER_B)),
                  _const_spec((2, INNER_B)),
                  _const_spec((1, INNER_B)),
                  _const_spec((INNER_B, D_MODEL)),
                  _const_spec((1, D_MODEL)),
                  pl.BlockSpec((bstep, H_B, DK_B, DV_B), lambda j: (j, 0, 0, 0))],
        out_specs=[pl.BlockSpec((rows, D_MODEL), lambda j: (0, 0)),
                   pl.BlockSpec((bstep, H_B, DK_B, DV_B), lambda j: (j, 0, 0, 0))],
        out_shape=[jax.ShapeDtypeStruct((rows, D_MODEL), F32),
                   jax.ShapeDtypeStruct((nb, H_B, DK_B, DV_B), F32)],
        scratch_shapes=[slab, slab, slab, slab, slab, slab, slab,
                        pltpu.VMEM((rows, INNER_B), F32)],
        compiler_params=pltpu.CompilerParams(dimension_semantics=("arbitrary",),
                                             vmem_limit_bytes=VMEM_LIMIT),
        name="hgrn_decode",
    )(x1, gain, w_in, lbl, o_gain, w_out, f_gain, s0)


def _gate_weights(w_r, w_i):
    per = GATE_SUPER // BLK_W

    def block_diag(w):
        w4 = w.reshape(N_SUPER, per, BLK_W, BLK_W)
        eye = jnp.eye(per, dtype=w.dtype)
        return jnp.einsum('saij,ac->saicj', w4, eye).reshape(N_SUPER, GATE_SUPER, GATE_SUPER)

    return jnp.concatenate([block_diag(w_r), block_diag(w_i)], axis=-1).astype(BF16)


def kernel(x_prompt, x_sample, state_lru_h, state_lru_conv, state_hgrn, norm_gain, a_w_in, a_conv_w, a_conv_b,
           a_w_r, a_b_r, a_w_i, a_b_i, a_lambda, a_w_out, b_w_in, b_lb_logits, b_o_gain, b_w_out, final_gain):
    assert norm_gain.shape[0] == 2 and a_w_in.shape[0] == 1 and b_w_in.shape[0] == 1
    pb, pt, _ = x_prompt.shape
    sb, st, _ = x_sample.shape
    halo_t = CONV_W - 1

    row = lambda p: p.reshape(1, -1)
    lru_params = (row(norm_gain[0]), a_w_in[0].astype(BF16), a_conv_w[0], row(a_conv_b[0]),
                  _gate_weights(a_w_r[0], a_w_i[0]), row(a_b_r[0]), row(a_b_i[0]), row(a_lambda[0]),
                  a_w_out[0].astype(BF16))
    hgrn_params = (row(norm_gain[1]), b_w_in[0].astype(BF16), b_lb_logits, row(b_o_gain[0]),
                   b_w_out[0].astype(BF16), row(final_gain))

    x1p, hp, bufp = _lru_layer(x_prompt, jnp.zeros((pb, W_LRU), F32), jnp.zeros((halo_t * pb, W_LRU), F32),
                               *lru_params, nb=pb, tt=64, batch_major=True)
    y_prompt, sp = _hgrn_prompt(x1p, *hgrn_params, tt=256)
    bufp = bufp.reshape(halo_t, pb, W_LRU).transpose(1, 0, 2)

    xs = x_sample.transpose(1, 0, 2).reshape(st * sb, D_MODEL)
    bufs0 = state_lru_conv[0].transpose(1, 0, 2).reshape(halo_t * sb, W_LRU)
    x1s, hs, bufs = _lru_layer(xs, state_lru_h[0], bufs0, *lru_params, nb=sb, tt=st, batch_major=False)
    ys, ss = _hgrn_decode(x1s, state_hgrn[0], *hgrn_params, nb=sb, tt=st, bstep=4)
    bufs = bufs.reshape(halo_t, sb, W_LRU).transpose(1, 0, 2)

    return (y_prompt, ys.reshape(sb, st, D_MODEL), hp[None], bufp[None], sp[None],
            hs[None], bufs[None], ss[None])
```

```python
import functools

import jax
import jax.numpy as jnp
from jax import lax
from jax.experimental import pallas as pl
from jax.experimental.pallas import tpu as pltpu

F32 = jnp.float32
BF16 = jnp.bfloat16

D_MODEL = 1024
W_LRU = 1536
N_BLK = 16
BLK_W = W_LRU // N_BLK
CONV_W = 4
LRU_C = 8.0
H_B = 8
DK_B = 128
DV_B = 128
INNER_B = H_B * DK_B
EPS = 1e-6

LANES = 128
GATE_SUPER = 384
N_SUPER = W_LRU // GATE_SUPER
LRU_PARTS = 2
CHUNK = 64
LOG_CHUNK = 6
VMEM_LIMIT = 56 * 1024 * 1024
MAX_LOCAL_EXPONENT = 75.0
LOG_FLOOR = -1e4


def _sigmoid_pair(x):
    th = 0.5 * jnp.tanh(0.5 * x)
    return 0.5 + th, 0.5 - th


def _silu(x):
    return x * (0.5 + 0.5 * jnp.tanh(0.5 * x))


def _rms_scale(x, gain):
    ms = jnp.mean(x * x, axis=-1, keepdims=True)
    return x * lax.rsqrt(ms + EPS) * gain


def _const_spec(shape):
    zeros = (0,) * len(shape)
    return pl.BlockSpec(shape, lambda *_: zeros, pipeline_mode=pl.Buffered(1))


def _lru_kernel(nb, tt, batch_major, x_ref, gain_ref, win_ref, cw_ref, cb_ref, wg_ref, br_ref, bi_ref,
                lam_ref, wout_ref, h0_ref, buf0_ref, x1_ref, hlast_ref, bufout_ref, *scratch):
    if batch_major:
        xs_s, os_s, xbuf_s, a_s, b_s, g_s, hc_s = scratch
    else:
        xbuf_s, a_s, b_s, g_s, hc_s = scratch
    rows = nb * tt
    halo = (CONV_W - 1) * nb
    n_lane_tiles = D_MODEL // LANES
    tp = tt // LRU_PARTS
    rp = nb * tp

    @pl.when(pl.program_id(0) == 0)
    def _():
        hc_s[...] = h0_ref[...]
        xbuf_s[0:halo, :] = buf0_ref[...]

    if batch_major:
        for b in range(nb):
            for j in range(n_lane_tiles):
                xs_s[j, pl.ds(b, tt, stride=nb), :] = x_ref[b, :, j * LANES:(j + 1) * LANES]

    def load_x(p):
        if batch_major:
            return jnp.concatenate([xs_s[j, p * rp:(p + 1) * rp, :] for j in range(n_lane_tiles)], axis=-1)
        return x_ref[p * rp:(p + 1) * rp, :]

    nlam = -lam_ref[...]
    softplus = jnp.maximum(nlam, 0.0) + jnp.log1p(jnp.exp(-jnp.abs(nlam)))
    rate = (-LRU_C) * softplus

    def pre(p):
        r0 = p * rp
        xn = _rms_scale(load_x(p), gain_ref[...]).astype(BF16)
        u = jnp.dot(xn, win_ref[...], preferred_element_type=F32)
        g_s[r0:r0 + rp, :] = _silu(u[:, W_LRU:])
        xbuf_s[halo + r0:halo + r0 + rp, :] = u[:, :W_LRU]
        xc = cb_ref[...] + cw_ref[0:1, :] * xbuf_s[r0:r0 + rp, :]
        for k in range(1, CONV_W):
            xc = xc + cw_ref[k:k + 1, :] * xbuf_s[r0 + k * nb:r0 + k * nb + rp, :]
        xcb = xc.astype(BF16)
        r_parts, i_parts = [], []
        for s in range(N_SUPER):
            gsi = jnp.dot(xcb[:, s * GATE_SUPER:(s + 1) * GATE_SUPER], wg_ref[s], preferred_element_type=F32)
            r_parts.append(gsi[:, :GATE_SUPER])
            i_parts.append(gsi[:, GATE_SUPER:])
        r, _ = _sigmoid_pair(jnp.concatenate(r_parts, axis=-1) + br_ref[...])
        ig, _ = _sigmoid_pair(jnp.concatenate(i_parts, axis=-1) + bi_ref[...])
        log_a = rate * r
        a = jnp.exp(log_a)
        one_m_a2 = -jnp.tanh(log_a) * (a * a + 1.0)
        a_s[r0:r0 + rp, :] = a
        root = jnp.where(one_m_a2 > 0.0, one_m_a2 * lax.rsqrt(one_m_a2), 0.0)
        b_s[r0:r0 + rp, :] = root * (ig * xc)

    def scan(p, h):
        for t in range(p * tp, (p + 1) * tp):
            h = a_s[t * nb:(t + 1) * nb, :] * h + b_s[t * nb:(t + 1) * nb, :]
            b_s[t * nb:(t + 1) * nb, :] = h
        return h

    def post(p):
        r0 = p * rp
        y = (b_s[r0:r0 + rp, :] * g_s[r0:r0 + rp, :]).astype(BF16)
        out = load_x(p) + jnp.dot(y, wout_ref[...], preferred_element_type=F32)
        if batch_major:
            for j in range(n_lane_tiles):
                os_s[j, r0:r0 + rp, :] = out[:, j * LANES:(j + 1) * LANES]
            for b in range(nb):
                x1_ref[b, p * tp:(p + 1) * tp, :] = jnp.concatenate(
                    [os_s[j, pl.ds(r0 + b, tp, stride=nb), :] for j in range(n_lane_tiles)], axis=-1)
        else:
            x1_ref[r0:r0 + rp, :] = out

    for p in range(LRU_PARTS):
        pre(p)
    new_halo = xbuf_s[rows:rows + halo, :]
    bufout_ref[...] = new_halo
    xbuf_s[0:halo, :] = new_halo
    h = hc_s[...]
    for p in range(LRU_PARTS):
        h = scan(p, h)
        post(p)
    hc_s[...] = h
    hlast_ref[...] = h


def _lru_layer(x, h0, buf0, gain, w_in, conv_w, conv_b, wg, b_r, b_i, lam, w_out, *, nb, tt, batch_major):
    if batch_major:
        total_t = x.shape[1]
        x_spec = pl.BlockSpec((nb, tt, D_MODEL), lambda i: (0, i, 0))
        x_shape = jax.ShapeDtypeStruct((nb, total_t, D_MODEL), F32)
    else:
        total_t = x.shape[0] // nb
        x_spec = pl.BlockSpec((nb * tt, D_MODEL), lambda i: (i, 0))
        x_shape = jax.ShapeDtypeStruct((nb * total_t, D_MODEL), F32)
    rows = nb * tt
    halo = (CONV_W - 1) * nb
    scratch = []
    if batch_major:
        scratch += [pltpu.VMEM((D_MODEL // LANES, rows, LANES), F32),
                    pltpu.VMEM((D_MODEL // LANES, rows, LANES), F32)]
    scratch += [pltpu.VMEM((rows + halo, W_LRU), F32),
                pltpu.VMEM((rows, W_LRU), F32),
                pltpu.VMEM((rows, W_LRU), F32),
                pltpu.VMEM((rows, W_LRU), F32),
                pltpu.VMEM((nb, W_LRU), F32)]
    return pl.pallas_call(
        functools.partial(_lru_kernel, nb, tt, batch_major),
        grid=(total_t // tt,),
        in_specs=[x_spec,
                  _const_spec((1, D_MODEL)),
                  _const_spec((D_MODEL, 2 * W_LRU)),
                  _const_spec((CONV_W, W_LRU)),
                  _const_spec((1, W_LRU)),
                  _const_spec((N_SUPER, GATE_SUPER, 2 * GATE_SUPER)),
                  _const_spec((1, W_LRU)),
                  _const_spec((1, W_LRU)),
                  _const_spec((1, W_LRU)),
                  _const_spec((W_LRU, D_MODEL)),
                  _const_spec((nb, W_LRU)),
                  _const_spec((halo, W_LRU))],
        out_specs=[x_spec,
                   pl.BlockSpec((nb, W_LRU), lambda i: (0, 0)),
                   pl.BlockSpec((halo, W_LRU), lambda i: (0, 0))],
        out_shape=[x_shape,
                   jax.ShapeDtypeStruct((nb, W_LRU), F32),
                   jax.ShapeDtypeStruct((halo, W_LRU), F32)],
        scratch_shapes=scratch,
        compiler_params=pltpu.CompilerParams(dimension_semantics=("arbitrary",),
                                             vmem_limit_bytes=VMEM_LIMIT),
        name="lru_layer_bm" if batch_major else "lru_layer_tm",
    )(x, gain, w_in, conv_w, conv_b, wg, b_r, b_i, lam, w_out, h0, buf0)


def _lower_bound(lbl_ref):
    l0 = lbl_ref[0:1, :]
    l1 = lbl_ref[1:2, :]
    mx = jnp.maximum(l0, l1)
    e0 = jnp.exp(l0 - mx)
    e1 = jnp.exp(l1 - mx)
    return e1 / (e0 + e1)


def _hgrn_front(x, gain_ref, win_ref, lbl_ref):
    xn = _rms_scale(x, gain_ref[...]).astype(BF16)
    n = INNER_B
    q = _silu(jnp.dot(xn, win_ref[:, 0:n], preferred_element_type=F32))
    f = jnp.dot(xn, win_ref[:, n:2 * n], preferred_element_type=F32)
    v = jnp.dot(xn, win_ref[:, 2 * n:3 * n], preferred_element_type=F32)
    gs = _silu(jnp.dot(xn, win_ref[:, 3 * n:4 * n], preferred_element_type=F32))
    lb = _lower_bound(lbl_ref)
    sig, nsig = _sigmoid_pair(f)
    kk = (1.0 - lb) * nsig
    g = lb + (1.0 - lb) * sig
    lg = jnp.maximum(jnp.log(g), LOG_FLOOR)
    return q, kk, v, lg, gs


def _hgrn_back(x, o, gs, og_ref, wout_ref, fg_ref):
    parts = []
    for h in range(H_B):
        oh = o[:, h * DV_B:(h + 1) * DV_B]
        ms = jnp.mean(oh * oh, axis=-1, keepdims=True)
        parts.append(oh * lax.rsqrt(ms + EPS))
    on = jnp.concatenate(parts, axis=-1) * og_ref[...]
    y = (on * gs).astype(BF16)
    x2 = x + jnp.dot(y, wout_ref[...], preferred_element_type=F32)
    return _rms_scale(x2, fg_ref[...])


def _nt(a, b):
    return lax.dot_general(a, b, (((1,), (1,)), ((), ())), preferred_element_type=F32)


def _tn(a, b):
    return lax.dot_general(a, b, (((0,), (0,)), ((), ())), preferred_element_type=F32)


def _hgrn_prompt_kernel(tt, x_ref, gain_ref, win_ref, lbl_ref, og_ref, wout_ref, fg_ref,
                        y_ref, st_ref, q_s, k_s, v_s, cum_s, o_s, st_s):
    i = pl.program_id(1)
    n_chunks = tt // CHUNK
    half = CHUNK // 2

    @pl.when(i == 0)
    def _():
        st_s[...] = jnp.zeros_like(st_s)

    x = x_ref[...]
    q, kk, v, lg, gs = _hgrn_front(x, gain_ref, win_ref, lbl_ref)
    q_s[...] = q
    k_s[...] = kk
    v_s[...] = v.astype(BF16)

    row_c = lax.broadcasted_iota(jnp.int32, (CHUNK, CHUNK), 0)
    col_c = lax.broadcasted_iota(jnp.int32, (CHUNK, CHUNK), 1)
    tri = (row_c >= col_c).astype(BF16)
    tri3 = jnp.concatenate([tri, tri, tri], axis=-1)

    h1 = lg.astype(BF16)
    r1 = lg - h1.astype(F32)
    h2 = r1.astype(BF16)
    h3 = (r1 - h2.astype(F32)).astype(BF16)
    worst = jnp.zeros((1, INNER_B), F32)
    for c in range(n_chunks):
        rs = slice(c * CHUNK, (c + 1) * CHUNK)
        cum = jnp.dot(tri3, jnp.concatenate([h1[rs], h2[rs], h3[rs]], axis=0), preferred_element_type=F32)
        cum_s[rs, :] = cum
        mid = cum[half - 1:half, :]
        worst = jnp.maximum(worst, jnp.maximum(-mid, mid - cum[CHUNK - 1:CHUNK, :]))
    local_ok = jnp.max(worst) <= MAX_LOCAL_EXPONENT

    def head_update(h, base, att, qe_b, ke_b, d):
        sl = slice(h * DK_B, (h + 1) * DK_B)
        vh = v_s[pl.ds(base, CHUNK), sl]
        st = st_s[h]
        o = _nt(qe_b[:, sl], st.astype(BF16)) + jnp.dot(att.astype(BF16), vh, preferred_element_type=F32)
        st_s[h] = d[:, sl] * st + _tn(vh, ke_b[:, sl])
        o_s[pl.ds(base, CHUNK), sl] = o

    def chunk_common(base):
        cu = cum_s[pl.ds(base, CHUNK), :]
        tail = cum_s[pl.ds(base + CHUNK - 8, 8), :]
        clast = tail[7:8, :]
        qf = q_s[pl.ds(base, CHUNK), :]
        kf = k_s[pl.ds(base, CHUNK), :]
        qe = qf * jnp.exp(cu)
        ke_b = (kf * jnp.exp(clast - cu)).astype(BF16)
        return cu, qf, kf, qe, ke_b, jnp.exp(clast)

    def fast_chunk(c, carry):
        base = pl.multiple_of(c * CHUNK, CHUNK)
        cu, qf, kf, qe, ke_b, d = chunk_common(base)
        cmid = cum_s[pl.ds(base + half - 8, 8), :][7:8, :]
        right = lax.broadcasted_iota(jnp.int32, (CHUNK, INNER_B), 0) >= half
        da = cu - cmid
        za = jnp.where(right, qf, kf) * jnp.exp(-jnp.abs(da))
        kl_b = (kf * jnp.exp(jnp.where(right, -da, -cu))).astype(BF16)
        ql_b = jnp.where(right, za, qe).astype(BF16)
        za_b = za.astype(BF16)
        qe_b = qe.astype(BF16)
        cross = (row_c >= half) & (col_c < half)
        local = (row_c >= col_c) & ((row_c >= half) == (col_c >= half))
        for h in range(H_B):
            sl = slice(h * DK_B, (h + 1) * DK_B)
            pa = _nt(za_b[:, sl], za_b[:, sl])
            pb = _nt(ql_b[:, sl], kl_b[:, sl])
            att = jnp.where(cross, pa, jnp.where(local, pb, 0.0))
            head_update(h, base, att, qe_b, ke_b, d)
        return carry

    def slow_chunk(c, carry):
        base = pl.multiple_of(c * CHUNK, CHUNK)
        cu_all, qf, kf, qe, ke_b, d = chunk_common(base)
        qe_b = qe.astype(BF16)
        xor_c = row_c ^ col_c
        row_l = lax.broadcasted_iota(jnp.int32, (CHUNK, LANES), 0)
        sub8 = lax.broadcasted_iota(jnp.int32, (8, LANES), 0)
        for h in range(H_B):
            sl = slice(h * DK_B, (h + 1) * DK_B)
            cu = cu_all[:, sl]
            qh = qf[:, sl]
            kh = kf[:, sl]

            def ref_row(r):
                grp = cu[(r // 8) * 8:(r // 8) * 8 + 8, :]
                return grp[r % 8:r % 8 + 1, :]

            att = jnp.zeros((CHUNK, CHUNK), F32)
            for lm in range(LOG_CHUNK):
                m = 1 << lm
                if lm >= 2:
                    cm = jnp.concatenate(
                        [jnp.broadcast_to(ref_row(blk * 2 * m + m - 1), (2 * m, LANES))
                         for blk in range(CHUNK // (2 * m))], axis=0)
                elif lm == 1:
                    cm = jnp.concatenate(
                        [jnp.where(sub8 < 4,
                                   jnp.broadcast_to(ref_row(blk * 8 + 1), (8, LANES)),
                                   jnp.broadcast_to(ref_row(blk * 8 + 5), (8, LANES)))
                         for blk in range(CHUNK // 8)], axis=0)
                else:
                    cm = jnp.where((row_l & 1) == 1, pltpu.roll(cu, 1, axis=0), cu)
                right = ((row_l >> lm) & 1) == 1
                z = (jnp.where(right, qh, kh) * jnp.exp(-jnp.abs(cu - cm))).astype(BF16)
                pair = ((xor_c >> lm) == 1) & (((row_c >> lm) & 1) == 1)
                att = jnp.where(pair, _nt(z, z), att)
            att = jnp.where(row_c == col_c, _nt(qh.astype(BF16), kh.astype(BF16)), att)
            head_update(h, base, att, qe_b, ke_b, d)
        return carry

    lax.cond(local_ok,
             lambda: lax.fori_loop(0, n_chunks, fast_chunk, 0, unroll=True),
             lambda: lax.fori_loop(0, n_chunks, slow_chunk, 0))

    y_ref[...] = _hgrn_back(x, o_s[...], gs, og_ref, wout_ref, fg_ref)

    @pl.when(i == pl.num_programs(1) - 1)
    def _():
        for h in range(H_B):
            st_ref[h] = st_s[h].T


def _hgrn_prompt(x1, gain, w_in, lbl, o_gain, w_out, f_gain, *, tt):
    nb, total_t, _ = x1.shape
    return pl.pallas_call(
        functools.partial(_hgrn_prompt_kernel, tt),
        grid=(nb, total_t // tt),
        in_specs=[pl.BlockSpec((None, tt, D_MODEL), lambda b, i: (b, i, 0)),
                  _const_spec((1, D_MODEL)),
                  _const_spec((D_MODEL, 4 * INNER_B)),
                  _const_spec((2, INNER_B)),
                  _const_spec((1, INNER_B)),
                  _const_spec((INNER_B, D_MODEL)),
                  _const_spec((1, D_MODEL))],
        out_specs=[pl.BlockSpec((None, tt, D_MODEL), lambda b, i: (b, i, 0)),
                   pl.BlockSpec((None, H_B, DK_B, DV_B), lambda b, i: (b, 0, 0, 0))],
        out_shape=[jax.ShapeDtypeStruct((nb, total_t, D_MODEL), F32),
                   jax.ShapeDtypeStruct((nb, H_B, DK_B, DV_B), F32)],
        scratch_shapes=[pltpu.VMEM((tt, INNER_B), F32),
                        pltpu.VMEM((tt, INNER_B), F32),
                        pltpu.VMEM((tt, INNER_B), BF16),
                        pltpu.VMEM((tt, INNER_B), F32),
                        pltpu.VMEM((tt, INNER_B), F32),
                        pltpu.VMEM((H_B, DV_B, DK_B), F32)],
        compiler_params=pltpu.CompilerParams(dimension_semantics=("arbitrary", "arbitrary"),
                                             vmem_limit_bytes=VMEM_LIMIT),
        name="hgrn_prompt",
    )(x1, gain, w_in, lbl, o_gain, w_out, f_gain)


def _hgrn_decode_kernel(nb, tt, bstep, x_ref, gain_ref, win_ref, lbl_ref, og_ref, wout_ref, fg_ref, s0_ref,
                        y_ref, snew_ref, qe_s, ke_s, dd_s, v_s, oi_s, gs_s, x_s, o_s):
    j = pl.program_id(0)
    rows = nb * tt

    def to_batch_major(dst, val, t):
        for h in range(H_B):
            dst[h, pl.ds(t, nb, stride=tt), :] = val[:, h * LANES:(h + 1) * LANES]

    @pl.when(j == 0)
    def _():
        x = x_ref[...]
        q, kk, v, lg, gs = _hgrn_front(x, gain_ref, win_ref, lbl_ref)
        ones_blk = jnp.ones((DK_B, DK_B), BF16)

        def slab(arr, t):
            return arr[t * nb:(t + 1) * nb, :]

        cum = [slab(lg, 0)]
        for t in range(1, tt):
            cum.append(cum[-1] + slab(lg, t))
        dd_s[...] = jnp.zeros_like(dd_s)
        d = jnp.exp(cum[tt - 1])
        d_hi = d.astype(BF16).astype(F32)
        to_batch_major(dd_s, d_hi, 0)
        to_batch_major(dd_s, d - d_hi, 1)
        for t in range(tt):
            qt = slab(q, t)
            acc = jnp.zeros((nb, INNER_B), F32)
            for s in range(t + 1):
                prod = qt * slab(kk, s)
                if s < t:
                    prod = prod * jnp.exp(cum[t] - cum[s])
                pb = prod.astype(BF16)
                att = jnp.concatenate(
                    [jnp.dot(pb[:, h * DK_B:(h + 1) * DK_B], ones_blk, preferred_element_type=F32)
                     for h in range(H_B)], axis=-1)
                acc = acc + att * slab(v, s)
            to_batch_major(oi_s, acc, t)
            to_batch_major(qe_s, qt * jnp.exp(cum[t]), t)
            to_batch_major(ke_s, slab(kk, t) * jnp.exp(cum[tt - 1] - cum[t]), t)
            to_batch_major(v_s, slab(v, t), t)
            to_batch_major(gs_s, slab(gs, t), t)
            to_batch_major(x_s, slab(x, t), t)

    row1 = lax.broadcasted_iota(jnp.int32, (8, LANES), 0)
    lane2 = lax.broadcasted_iota(jnp.int32, (8, 2 * DV_B), 1)
    row2 = lax.broadcasted_iota(jnp.int32, (8, 2 * DV_B), 0)
    own = (row1 < tt, row1 >= tt)
    ones_rows = tuple(
        jnp.where((lane2 >= DV_B) & (row2 >= lo) & (row2 < lo + 2), 1.0, 0.0).astype(BF16) for lo in (tt, 0))

    def pair_body(pi, carry):
        off = pl.multiple_of((j * (bstep // 2) + pi) * 8, 8)
        for h in range(H_B):
            qe = qe_s[h, pl.ds(off, 8), :].astype(BF16)
            ke = ke_s[h, pl.ds(off, 8), :]
            dd = pltpu.roll(dd_s[h, pl.ds(off, 8), :], tt, axis=0)
            vv = v_s[h, pl.ds(off, 8), :]
            inter = []
            for e in range(2):
                s0 = s0_ref[2 * pi + e, h]
                inter.append(jnp.dot(qe, s0.astype(BF16), preferred_element_type=F32))
                lhs = jnp.where(own[e], ke, dd).astype(BF16)
                ve = jnp.where(own[e], vv, 0.0).astype(BF16)
                rhs = jnp.concatenate([ve, jnp.zeros((8, DV_B), BF16)], axis=-1) + ones_rows[e]
                upd = lax.dot_general(lhs, rhs, (((0,), (0,)), ((), ())),
                                      preferred_element_type=F32)
                snew_ref[2 * pi + e, h] = upd[:, DV_B:] * s0 + upd[:, :DV_B]
            o_s[pl.ds(off, 8), h * DV_B:(h + 1) * DV_B] = (
                jnp.where(own[0], inter[0], inter[1]) + oi_s[h, pl.ds(off, 8), :])
        return carry

    lax.fori_loop(0, bstep // 2, pair_body, 0)

    @pl.when(j == pl.num_programs(0) - 1)
    def _():
        xb = jnp.concatenate([x_s[h] for h in range(H_B)], axis=-1)
        gsb = jnp.concatenate([gs_s[h] for h in range(H_B)], axis=-1)
        y_ref[...] = _hgrn_back(xb, o_s[...], gsb, og_ref, wout_ref, fg_ref)


def _hgrn_decode(x1, s0, gain, w_in, lbl, o_gain, w_out, f_gain, *, nb, tt, bstep):
    rows = nb * tt
    slab = pltpu.VMEM((H_B, rows, LANES), F32)
    return pl.pallas_call(
        functools.partial(_hgrn_decode_kernel, nb, tt, bstep),
        grid=(nb // bstep,),
        in_specs=[_const_spec((rows, D_MODEL)),
                  _const_spec((1, D_MODEL)),
                  _const_spec((D_MODEL, 4 * INNER_B)),
                  _const_spec((2, INNER_B)),
                  _const_spec((1, INNER_B)),
                  _const_spec((INNER_B, D_MODEL)),
                  _const_spec((1, D_MODEL)),
                  pl.BlockSpec((bstep, H_B, DK_B, DV_B), lambda j: (j, 0, 0, 0))],
        out_specs=[pl.BlockSpec((rows, D_MODEL), lambda j: (0, 0)),
                   pl.BlockSpec((bstep, H_B, DK_B, DV_B), lambda j: (j, 0, 0, 0))],
        out_shape=[jax.ShapeDtypeStruct((rows, D_MODEL), F32),
                   jax.ShapeDtypeStruct((nb, H_B, DK_B, DV_B), F32)],
        scratch_shapes=[slab, slab, slab, slab, slab, slab, slab,
                        pltpu.VMEM((rows, INNER_B), F32)],
        compiler_params=pltpu.CompilerParams(dimension_semantics=("arbitrary",),
                                             vmem_limit_bytes=VMEM_LIMIT),
        name="hgrn_decode",
    )(x1, gain, w_in, lbl, o_gain, w_out, f_gain, s0)


def _gate_weights(w_r, w_i):
    per = GATE_SUPER // BLK_W

    def block_diag(w):
        w4 = w.reshape(N_SUPER, per, BLK_W, BLK_W)
        eye = jnp.eye(per, dtype=w.dtype)
        return jnp.einsum('saij,ac->saicj', w4, eye).reshape(N_SUPER, GATE_SUPER, GATE_SUPER)

    return jnp.concatenate([block_diag(w_r), block_diag(w_i)], axis=-1).astype(BF16)


def kernel(x_prompt, x_sample, state_lru_h, state_lru_conv, state_hgrn, norm_gain, a_w_in, a_conv_w, a_conv_b,
           a_w_r, a_b_r, a_w_i, a_b_i, a_lambda, a_w_out, b_w_in, b_lb_logits, b_o_gain, b_w_out, final_gain):
    assert norm_gain.shape[0] == 2 and a_w_in.shape[0] == 1 and b_w_in.shape[0] == 1
    pb, pt, _ = x_prompt.shape
    sb, st, _ = x_sample.shape
    halo_t = CONV_W - 1

    row = lambda p: p.reshape(1, -1)
    lru_params = (row(norm_gain[0]), a_w_in[0].astype(BF16), a_conv_w[0], row(a_conv_b[0]),
                  _gate_weights(a_w_r[0], a_w_i[0]), row(a_b_r[0]), row(a_b_i[0]), row(a_lambda[0]),
                  a_w_out[0].astype(BF16))
    hgrn_params = (row(norm_gain[1]), b_w_in[0].astype(BF16), b_lb_logits, row(b_o_gain[0]),
                   b_w_out[0].astype(BF16), row(final_gain))

    x1p, hp, bufp = _lru_layer(x_prompt, jnp.zeros((pb, W_LRU), F32), jnp.zeros((halo_t * pb, W_LRU), F32),
                               *lru_params, nb=pb, tt=64, batch_major=True)
    y_prompt, sp = _hgrn_prompt(x1p, *hgrn_params, tt=256)
    bufp = bufp.reshape(halo_t, pb, W_LRU).transpose(1, 0, 2)

    xs = x_sample.transpose(1, 0, 2).reshape(st * sb, D_MODEL)
    bufs0 = state_lru_conv[0].transpose(1, 0, 2).reshape(halo_t * sb, W_LRU)
    x1s, hs, bufs = _lru_layer(xs, state_lru_h[0], bufs0, *lru_params, nb=sb, tt=st, batch_major=False)
    ys, ss = _hgrn_decode(x1s, state_hgrn[0], *hgrn_params, nb=sb, tt=st, bstep=4)
    bufs = bufs.reshape(halo_t, sb, W_LRU).transpose(1, 0, 2)

    return (y_prompt, ys.reshape(sb, st, D_MODEL), hp[None], bufp[None], sp[None],
            hs[None], bufs[None], ss[None])
```

```python
import functools

import jax
import jax.numpy as jnp
from jax import lax
from jax.experimental import pallas as pl
from jax.experimental.pallas import tpu as pltpu

F32 = jnp.float32
BF16 = jnp.bfloat16

D_MODEL = 1024
W_LRU = 1536
N_BLK = 16
BLK_W = W_LRU // N_BLK
CONV_W = 4
LRU_C = 8.0
H_B = 8
DK_B = 128
DV_B = 128
INNER_B = H_B * DK_B
EPS = 1e-6

LANES = 128
GATE_SUPER = 384
N_SUPER = W_LRU // GATE_SUPER
LRU_PARTS = 2
HGRN_PARTS = 2
CHUNK = 64
LOG_CHUNK = 6
VMEM_LIMIT = 56 * 1024 * 1024
MAX_LOCAL_EXPONENT = 75.0
LOG_FLOOR = -1e4


def _sigmoid_pair(x):
    th = 0.5 * jnp.tanh(0.5 * x)
    return 0.5 + th, 0.5 - th


def _silu(x):
    return x * (0.5 + 0.5 * jnp.tanh(0.5 * x))


def _rms_scale(x, gain):
    ms = jnp.mean(x * x, axis=-1, keepdims=True)
    return x * lax.rsqrt(ms + EPS) * gain


def _const_spec(shape):
    zeros = (0,) * len(shape)
    return pl.BlockSpec(shape, lambda *_: zeros, pipeline_mode=pl.Buffered(1))


def _gate_k_ranges():
    ranges = []
    for j in range(GATE_SUPER // LANES):
        first_blk = (j * LANES) // BLK_W
        last_blk = ((j + 1) * LANES - 1) // BLK_W
        lo = (first_blk * BLK_W) // LANES * LANES
        hi = -(-((last_blk + 1) * BLK_W) // LANES) * LANES
        ranges.append((lo, hi))
    return ranges


GATE_K_RANGES = _gate_k_ranges()


def _lru_kernel(nb, tt, batch_major, x_ref, gain_ref, win_ref, cw_ref, cb_ref, wg0_ref, wg1_ref, wg2_ref,
                br_ref, bi_ref, lam_ref, wout_ref, h0_ref, buf0_ref, x1_ref, hlast_ref, bufout_ref, *scratch):
    wg_refs = (wg0_ref, wg1_ref, wg2_ref)
    if batch_major:
        xs_s, os_s, xbuf_s, a_s, b_s, g_s, hc_s = scratch
    else:
        xbuf_s, a_s, b_s, g_s, hc_s = scratch
    rows = nb * tt
    halo = (CONV_W - 1) * nb
    n_lane_tiles = D_MODEL // LANES
    tp = tt // LRU_PARTS
    rp = nb * tp

    @pl.when(pl.program_id(0) == 0)
    def _():
        hc_s[...] = h0_ref[...]
        xbuf_s[0:halo, :] = buf0_ref[...]

    if batch_major:
        for b in range(nb):
            for j in range(n_lane_tiles):
                xs_s[j, pl.ds(b, tt, stride=nb), :] = x_ref[b, :, j * LANES:(j + 1) * LANES]

    def load_x(p):
        if batch_major:
            return jnp.concatenate([xs_s[j, p * rp:(p + 1) * rp, :] for j in range(n_lane_tiles)], axis=-1)
        return x_ref[p * rp:(p + 1) * rp, :]

    nlam = -lam_ref[...]
    softplus = jnp.maximum(nlam, 0.0) + jnp.log1p(jnp.exp(-jnp.abs(nlam)))
    rate = (-LRU_C) * softplus

    def pre(p):
        r0 = p * rp
        xn = _rms_scale(load_x(p), gain_ref[...]).astype(BF16)
        u = jnp.dot(xn, win_ref[...], preferred_element_type=F32)
        g_s[r0:r0 + rp, :] = _silu(u[:, W_LRU:])
        xbuf_s[halo + r0:halo + r0 + rp, :] = u[:, :W_LRU]
        xc = cb_ref[...] + cw_ref[0:1, :] * xbuf_s[r0:r0 + rp, :]
        for k in range(1, CONV_W):
            xc = xc + cw_ref[k:k + 1, :] * xbuf_s[r0 + k * nb:r0 + k * nb + rp, :]
        xcb = xc.astype(BF16)
        r_parts, i_parts = [], []
        for s in range(N_SUPER):
            for (lo, hi), wg_ref in zip(GATE_K_RANGES, wg_refs):
                gsi = jnp.dot(xcb[:, s * GATE_SUPER + lo:s * GATE_SUPER + hi], wg_ref[s],
                              preferred_element_type=F32)
                r_parts.append(gsi[:, :LANES])
                i_parts.append(gsi[:, LANES:])
        r, _ = _sigmoid_pair(jnp.concatenate(r_parts, axis=-1) + br_ref[...])
        ig, _ = _sigmoid_pair(jnp.concatenate(i_parts, axis=-1) + bi_ref[...])
        log_a = rate * r
        a = jnp.exp(log_a)
        one_m_a2 = -jnp.tanh(log_a) * (a * a + 1.0)
        a_s[r0:r0 + rp, :] = a
        root = jnp.where(one_m_a2 > 0.0, one_m_a2 * lax.rsqrt(one_m_a2), 0.0)
        b_s[r0:r0 + rp, :] = root * (ig * xc)

    def scan(p, h):
        for t in range(p * tp, (p + 1) * tp):
            h = a_s[t * nb:(t + 1) * nb, :] * h + b_s[t * nb:(t + 1) * nb, :]
            b_s[t * nb:(t + 1) * nb, :] = h
        return h

    def post(p):
        r0 = p * rp
        y = (b_s[r0:r0 + rp, :] * g_s[r0:r0 + rp, :]).astype(BF16)
        out = load_x(p) + jnp.dot(y, wout_ref[...], preferred_element_type=F32)
        if batch_major:
            for j in range(n_lane_tiles):
                os_s[j, r0:r0 + rp, :] = out[:, j * LANES:(j + 1) * LANES]
            for b in range(nb):
                x1_ref[b, p * tp:(p + 1) * tp, :] = jnp.concatenate(
                    [os_s[j, pl.ds(r0 + b, tp, stride=nb), :] for j in range(n_lane_tiles)], axis=-1)
        else:
            x1_ref[r0:r0 + rp, :] = out

    for p in range(LRU_PARTS):
        pre(p)
    new_halo = xbuf_s[rows:rows + halo, :]
    bufout_ref[...] = new_halo
    xbuf_s[0:halo, :] = new_halo
    h = hc_s[...]
    for p in range(LRU_PARTS):
        h = scan(p, h)
        post(p)
    hc_s[...] = h
    hlast_ref[...] = h


def _lru_layer(x, h0, buf0, gain, w_in, conv_w, conv_b, wg0, wg1, wg2, b_r, b_i, lam, w_out, *, nb, tt, batch_major):
    if batch_major:
        total_t = x.shape[1]
        x_spec = pl.BlockSpec((nb, tt, D_MODEL), lambda i: (0, i, 0))
        x_shape = jax.ShapeDtypeStruct((nb, total_t, D_MODEL), F32)
    else:
        total_t = x.shape[0] // nb
        x_spec = pl.BlockSpec((nb * tt, D_MODEL), lambda i: (i, 0))
        x_shape = jax.ShapeDtypeStruct((nb * total_t, D_MODEL), F32)
    rows = nb * tt
    halo = (CONV_W - 1) * nb
    scratch = []
    if batch_major:
        scratch += [pltpu.VMEM((D_MODEL // LANES, rows, LANES), F32),
                    pltpu.VMEM((D_MODEL // LANES, rows, LANES), F32)]
    scratch += [pltpu.VMEM((rows + halo, W_LRU), F32),
                pltpu.VMEM((rows, W_LRU), F32),
                pltpu.VMEM((rows, W_LRU), F32),
                pltpu.VMEM((rows, W_LRU), F32),
                pltpu.VMEM((nb, W_LRU), F32)]
    return pl.pallas_call(
        functools.partial(_lru_kernel, nb, tt, batch_major),
        grid=(total_t // tt,),
        in_specs=[x_spec,
                  _const_spec((1, D_MODEL)),
                  _const_spec((D_MODEL, 2 * W_LRU)),
                  _const_spec((CONV_W, W_LRU)),
                  _const_spec((1, W_LRU)),
                  *[_const_spec((N_SUPER, hi - lo, 2 * LANES)) for lo, hi in GATE_K_RANGES],
                  _const_spec((1, W_LRU)),
                  _const_spec((1, W_LRU)),
                  _const_spec((1, W_LRU)),
                  _const_spec((W_LRU, D_MODEL)),
                  _const_spec((nb, W_LRU)),
                  _const_spec((halo, W_LRU))],
        out_specs=[x_spec,
                   pl.BlockSpec((nb, W_LRU), lambda i: (0, 0)),
                   pl.BlockSpec((halo, W_LRU), lambda i: (0, 0))],
        out_shape=[x_shape,
                   jax.ShapeDtypeStruct((nb, W_LRU), F32),
                   jax.ShapeDtypeStruct((halo, W_LRU), F32)],
        scratch_shapes=scratch,
        compiler_params=pltpu.CompilerParams(dimension_semantics=("arbitrary",),
                                             vmem_limit_bytes=VMEM_LIMIT),
        name="lru_layer_bm" if batch_major else "lru_layer_tm",
    )(x, gain, w_in, conv_w, conv_b, wg0, wg1, wg2, b_r, b_i, lam, w_out, h0, buf0)


def _lower_bound(lbl_ref):
    l0 = lbl_ref[0:1, :]
    l1 = lbl_ref[1:2, :]
    mx = jnp.maximum(l0, l1)
    e0 = jnp.exp(l0 - mx)
    e1 = jnp.exp(l1 - mx)
    return e1 / (e0 + e1)


def _hgrn_front(x, gain_ref, win_ref, lbl_ref):
    xn = _rms_scale(x, gain_ref[...]).astype(BF16)
    n = INNER_B
    q = _silu(jnp.dot(xn, win_ref[:, 0:n], preferred_element_type=F32))
    f = jnp.dot(xn, win_ref[:, n:2 * n], preferred_element_type=F32)
    v = jnp.dot(xn, win_ref[:, 2 * n:3 * n], preferred_element_type=F32)
    gs = _silu(jnp.dot(xn, win_ref[:, 3 * n:4 * n], preferred_element_type=F32))
    lb = _lower_bound(lbl_ref)
    sig, nsig = _sigmoid_pair(f)
    kk = (1.0 - lb) * nsig
    g = lb + (1.0 - lb) * sig
    lg = jnp.maximum(jnp.log(g), LOG_FLOOR)
    return q, kk, v, lg, gs


def _hgrn_back(x, o, gs, og_ref, wout_ref, fg_ref):
    parts = []
    for h in range(H_B):
        oh = o[:, h * DV_B:(h + 1) * DV_B]
        ms = jnp.mean(oh * oh, axis=-1, keepdims=True)
        parts.append(oh * lax.rsqrt(ms + EPS))
    on = jnp.concatenate(parts, axis=-1) * og_ref[...]
    y = (on * gs).astype(BF16)
    x2 = x + jnp.dot(y, wout_ref[...], preferred_element_type=F32)
    return _rms_scale(x2, fg_ref[...])


def _nt(a, b):
    return lax.dot_general(a, b, (((1,), (1,)), ((), ())), preferred_element_type=F32)


def _tn(a, b):
    return lax.dot_general(a, b, (((0,), (0,)), ((), ())), preferred_element_type=F32)


def _hgrn_prompt_kernel(tt, x_ref, gain_ref, win_ref, lbl_ref, og_ref, wout_ref, fg_ref,
                        y_ref, st_ref, q_s, k_s, v_s, cum_s, gs_s, o_s, st_s, st0_s):
    i = pl.program_id(1)
    n_chunks = tt // CHUNK
    half = CHUNK // 2
    rp = tt // HGRN_PARTS
    chunks_per_part = rp // CHUNK

    @pl.when(i == 0)
    def _():
        st_s[...] = jnp.zeros_like(st_s)

    st0_s[...] = st_s[...]

    row_c = lax.broadcasted_iota(jnp.int32, (CHUNK, CHUNK), 0)
    col_c = lax.broadcasted_iota(jnp.int32, (CHUNK, CHUNK), 1)
    tri = (row_c >= col_c).astype(BF16)
    tri3 = jnp.concatenate([tri, tri, tri], axis=-1)

    def front(p):
        r0 = p * rp
        q, kk, v, lg, gs = _hgrn_front(x_ref[r0:r0 + rp, :], gain_ref, win_ref, lbl_ref)
        q_s[r0:r0 + rp, :] = q
        k_s[r0:r0 + rp, :] = kk
        v_s[r0:r0 + rp, :] = v.astype(BF16)
        gs_s[r0:r0 + rp, :] = gs
        h1 = lg.astype(BF16)
        r1 = lg - h1.astype(F32)
        h2 = r1.astype(BF16)
        h3 = (r1 - h2.astype(F32)).astype(BF16)
        worst = jnp.zeros((1, INNER_B), F32)
        for c in range(chunks_per_part):
            rs = slice(c * CHUNK, (c + 1) * CHUNK)
            cum = jnp.dot(tri3, jnp.concatenate([h1[rs], h2[rs], h3[rs]], axis=0), preferred_element_type=F32)
            cum_s[r0 + c * CHUNK:r0 + (c + 1) * CHUNK, :] = cum
            mid = cum[half - 1:half, :]
            worst = jnp.maximum(worst, jnp.maximum(-mid, mid - cum[CHUNK - 1:CHUNK, :]))
        return worst

    def back(p):
        r0 = p * rp
        y_ref[r0:r0 + rp, :] = _hgrn_back(x_ref[r0:r0 + rp, :], o_s[r0:r0 + rp, :], gs_s[r0:r0 + rp, :],
                                           og_ref, wout_ref, fg_ref)

    def head_update(h, base, att, qe_b, ke_b, d):
        sl = slice(h * DK_B, (h + 1) * DK_B)
        vh = v_s[pl.ds(base, CHUNK), sl]
        st = st_s[h]
        o = _nt(qe_b[:, sl], st.astype(BF16)) + jnp.dot(att.astype(BF16), vh, preferred_element_type=F32)
        st_s[h] = d[:, sl] * st + _tn(vh, ke_b[:, sl])
        o_s[pl.ds(base, CHUNK), sl] = o

    def chunk_common(base):
        cu = cum_s[pl.ds(base, CHUNK), :]
        tail = cum_s[pl.ds(base + CHUNK - 8, 8), :]
        clast = tail[7:8, :]
        qf = q_s[pl.ds(base, CHUNK), :]
        kf = k_s[pl.ds(base, CHUNK), :]
        qe = qf * jnp.exp(cu)
        ke_b = (kf * jnp.exp(clast - cu)).astype(BF16)
        return cu, qf, kf, qe, ke_b, jnp.exp(clast)

    def fast_chunk(base):
        cu, qf, kf, qe, ke_b, d = chunk_common(base)
        cmid = cum_s[pl.ds(base + half - 8, 8), :][7:8, :]
        right = lax.broadcasted_iota(jnp.int32, (CHUNK, INNER_B), 0) >= half
        da = cu - cmid
        za = jnp.where(right, qf, kf) * jnp.exp(-jnp.abs(da))
        kl_b = (kf * jnp.exp(jnp.where(right, -da, -cu))).astype(BF16)
        ql_b = jnp.where(right, za, qe).astype(BF16)
        za_b = za.astype(BF16)
        qe_b = qe.astype(BF16)
        cross = (row_c >= half) & (col_c < half)
        local = (row_c >= col_c) & ((row_c >= half) == (col_c >= half))
        for h in range(H_B):
            sl = slice(h * DK_B, (h + 1) * DK_B)
            pa = _nt(za_b[:, sl], za_b[:, sl])
            pb = _nt(ql_b[:, sl], kl_b[:, sl])
            att = jnp.where(cross, pa, jnp.where(local, pb, 0.0))
            head_update(h, base, att, qe_b, ke_b, d)

    def slow_chunk(c, carry):
        base = pl.multiple_of(c * CHUNK, CHUNK)
        cu_all, qf, kf, qe, ke_b, d = chunk_common(base)
        qe_b = qe.astype(BF16)
        xor_c = row_c ^ col_c
        row_l = lax.broadcasted_iota(jnp.int32, (CHUNK, LANES), 0)
        sub8 = lax.broadcasted_iota(jnp.int32, (8, LANES), 0)
        for h in range(H_B):
            sl = slice(h * DK_B, (h + 1) * DK_B)
            cu = cu_all[:, sl]
            qh = qf[:, sl]
            kh = kf[:, sl]

            def ref_row(r):
                grp = cu[(r // 8) * 8:(r // 8) * 8 + 8, :]
                return grp[r % 8:r % 8 + 1, :]

            att = jnp.zeros((CHUNK, CHUNK), F32)
            for lm in range(LOG_CHUNK):
                m = 1 << lm
                if lm >= 2:
                    cm = jnp.concatenate(
                        [jnp.broadcast_to(ref_row(blk * 2 * m + m - 1), (2 * m, LANES))
                         for blk in range(CHUNK // (2 * m))], axis=0)
                elif lm == 1:
                    cm = jnp.concatenate(
                        [jnp.where(sub8 < 4,
                                   jnp.broadcast_to(ref_row(blk * 8 + 1), (8, LANES)),
                                   jnp.broadcast_to(ref_row(blk * 8 + 5), (8, LANES)))
                         for blk in range(CHUNK // 8)], axis=0)
                else:
                    cm = jnp.where((row_l & 1) == 1, pltpu.roll(cu, 1, axis=0), cu)
                right = ((row_l >> lm) & 1) == 1
                z = (jnp.where(right, qh, kh) * jnp.exp(-jnp.abs(cu - cm))).astype(BF16)
                pair = ((xor_c >> lm) == 1) & (((row_c >> lm) & 1) == 1)
                att = jnp.where(pair, _nt(z, z), att)
            att = jnp.where(row_c == col_c, _nt(qh.astype(BF16), kh.astype(BF16)), att)
            head_update(h, base, att, qe_b, ke_b, d)
        return carry

    worst = front(0)
    for p in range(1, HGRN_PARTS):
        worst = jnp.maximum(worst, front(p))
    for c in range(n_chunks):
        fast_chunk(c * CHUNK)
    for p in range(HGRN_PARTS):
        back(p)

    @pl.when(jnp.max(worst) > MAX_LOCAL_EXPONENT)
    def _():
        st_s[...] = st0_s[...]
        lax.fori_loop(0, n_chunks, slow_chunk, 0)
        for p in range(HGRN_PARTS):
            back(p)

    @pl.when(i == pl.num_programs(1) - 1)
    def _():
        for h in range(H_B):
            st_ref[h] = st_s[h].T


def _hgrn_prompt(x1, gain, w_in, lbl, o_gain, w_out, f_gain, *, tt):
    nb, total_t, _ = x1.shape
    return pl.pallas_call(
        functools.partial(_hgrn_prompt_kernel, tt),
        grid=(nb, total_t // tt),
        in_specs=[pl.BlockSpec((None, tt, D_MODEL), lambda b, i: (b, i, 0)),
                  _const_spec((1, D_MODEL)),
                  _const_spec((D_MODEL, 4 * INNER_B)),
                  _const_spec((2, INNER_B)),
                  _const_spec((1, INNER_B)),
                  _const_spec((INNER_B, D_MODEL)),
                  _const_spec((1, D_MODEL))],
        out_specs=[pl.BlockSpec((None, tt, D_MODEL), lambda b, i: (b, i, 0)),
                   pl.BlockSpec((None, H_B, DK_B, DV_B), lambda b, i: (b, 0, 0, 0))],
        out_shape=[jax.ShapeDtypeStruct((nb, total_t, D_MODEL), F32),
                   jax.ShapeDtypeStruct((nb, H_B, DK_B, DV_B), F32)],
        scratch_shapes=[pltpu.VMEM((tt, INNER_B), F32),
                        pltpu.VMEM((tt, INNER_B), F32),
                        pltpu.VMEM((tt, INNER_B), BF16),
                        pltpu.VMEM((tt, INNER_B), F32),
                        pltpu.VMEM((tt, INNER_B), F32),
                        pltpu.VMEM((tt, INNER_B), F32),
                        pltpu.VMEM((H_B, DV_B, DK_B), F32),
                        pltpu.VMEM((H_B, DV_B, DK_B), F32)],
        compiler_params=pltpu.CompilerParams(dimension_semantics=("arbitrary", "arbitrary"),
                                             vmem_limit_bytes=VMEM_LIMIT),
        name="hgrn_prompt",
    )(x1, gain, w_in, lbl, o_gain, w_out, f_gain)


def _hgrn_decode_kernel(nb, tt, bstep, x_ref, gain_ref, win_ref, lbl_ref, og_ref, wout_ref, fg_ref, s0_ref,
                        y_ref, snew_ref, qe_s, ke_s, dd_s, v_s, oi_s, gs_s, x_s, o_s):
    j = pl.program_id(0)
    rows = nb * tt

    def to_batch_major(dst, val, t):
        for h in range(H_B):
            dst[h, pl.ds(t, nb, stride=tt), :] = val[:, h * LANES:(h + 1) * LANES]

    @pl.when(j == 0)
    def _():
        x = x_ref[...]
        q, kk, v, lg, gs = _hgrn_front(x, gain_ref, win_ref, lbl_ref)
        ones_blk = jnp.ones((DK_B, DK_B), BF16)

        def slab(arr, t):
            return arr[t * nb:(t + 1) * nb, :]

        cum = [slab(lg, 0)]
        for t in range(1, tt):
            cum.append(cum[-1] + slab(lg, t))
        dd_s[...] = jnp.zeros_like(dd_s)
        d = jnp.exp(cum[tt - 1])
        d_hi = d.astype(BF16).astype(F32)
        to_batch_major(dd_s, d_hi, 0)
        to_batch_major(dd_s, d - d_hi, 1)
        for t in range(tt):
            qt = slab(q, t)
            acc = jnp.zeros((nb, INNER_B), F32)
            for s in range(t + 1):
                prod = qt * slab(kk, s)
                if s < t:
                    prod = prod * jnp.exp(cum[t] - cum[s])
                pb = prod.astype(BF16)
                att = jnp.concatenate(
                    [jnp.dot(pb[:, h * DK_B:(h + 1) * DK_B], ones_blk, preferred_element_type=F32)
                     for h in range(H_B)], axis=-1)
                acc = acc + att * slab(v, s)
            to_batch_major(oi_s, acc, t)
            to_batch_major(qe_s, qt * jnp.exp(cum[t]), t)
            to_batch_major(ke_s, slab(kk, t) * jnp.exp(cum[tt - 1] - cum[t]), t)
            to_batch_major(v_s, slab(v, t), t)
            to_batch_major(gs_s, slab(gs, t), t)
            to_batch_major(x_s, slab(x, t), t)

    row1 = lax.broadcasted_iota(jnp.int32, (8, LANES), 0)
    lane2 = lax.broadcasted_iota(jnp.int32, (8, 2 * DV_B), 1)
    row2 = lax.broadcasted_iota(jnp.int32, (8, 2 * DV_B), 0)
    own = (row1 < tt, row1 >= tt)
    ones_rows = tuple(
        jnp.where((lane2 >= DV_B) & (row2 >= lo) & (row2 < lo + 2), 1.0, 0.0).astype(BF16) for lo in (tt, 0))

    def pair_body(pi, carry):
        off = pl.multiple_of((j * (bstep // 2) + pi) * 8, 8)
        for h in range(H_B):
            qe = qe_s[h, pl.ds(off, 8), :].astype(BF16)
            ke = ke_s[h, pl.ds(off, 8), :]
            dd = pltpu.roll(dd_s[h, pl.ds(off, 8), :], tt, axis=0)
            vv = v_s[h, pl.ds(off, 8), :]
            inter = []
            for e in range(2):
                s0 = s0_ref[2 * pi + e, h]
                inter.append(jnp.dot(qe, s0.astype(BF16), preferred_element_type=F32))
                lhs = jnp.where(own[e], ke, dd).astype(BF16)
                ve = jnp.where(own[e], vv, 0.0).astype(BF16)
                rhs = jnp.concatenate([ve, jnp.zeros((8, DV_B), BF16)], axis=-1) + ones_rows[e]
                upd = lax.dot_general(lhs, rhs, (((0,), (0,)), ((), ())),
                                      preferred_element_type=F32)
                snew_ref[2 * pi + e, h] = upd[:, DV_B:] * s0 + upd[:, :DV_B]
            o_s[pl.ds(off, 8), h * DV_B:(h + 1) * DV_B] = (
                jnp.where(own[0], inter[0], inter[1]) + oi_s[h, pl.ds(off, 8), :])
        return carry

    lax.fori_loop(0, bstep // 2, pair_body, 0)

    @pl.when(j == pl.num_programs(0) - 1)
    def _():
        xb = jnp.concatenate([x_s[h] for h in range(H_B)], axis=-1)
        gsb = jnp.concatenate([gs_s[h] for h in range(H_B)], axis=-1)
        y_ref[...] = _hgrn_back(xb, o_s[...], gsb, og_ref, wout_ref, fg_ref)


def _hgrn_decode(x1, s0, gain, w_in, lbl, o_gain, w_out, f_gain, *, nb, tt, bstep):
    rows = nb * tt
    slab = pltpu.VMEM((H_B, rows, LANES), F32)
    return pl.pallas_call(
        functools.partial(_hgrn_decode_kernel, nb, tt, bstep),
        grid=(nb // bstep,),
        in_specs=[_const_spec((rows, D_MODEL)),
                  _const_spec((1, D_MODEL)),
                  _const_spec((D_MODEL, 4 * INNER_B)),
                  _const_spec((2, INNER_B)),
                  _const_spec((1, INNER_B)),
                  _const_spec((INNER_B, D_MODEL)),
                  _const_spec((1, D_MODEL)),
                  pl.BlockSpec((bstep, H_B, DK_B, DV_B), lambda j: (j, 0, 0, 0))],
        out_specs=[pl.BlockSpec((rows, D_MODEL), lambda j: (0, 0)),
                   pl.BlockSpec((bstep, H_B, DK_B, DV_B), lambda j: (j, 0, 0, 0))],
        out_shape=[jax.ShapeDtypeStruct((rows, D_MODEL), F32),
                   jax.ShapeDtypeStruct((nb, H_B, DK_B, DV_B), F32)],
        scratch_shapes=[slab, slab, slab, slab, slab, slab, slab,
                        pltpu.VMEM((rows, INNER_B), F32)],
        compiler_params=pltpu.CompilerParams(dimension_semantics=("arbitrary",),
                                             vmem_limit_bytes=VMEM_LIMIT),
        name="hgrn_decode",
    )(x1, gain, w_in, lbl, o_gain, w_out, f_gain, s0)


def _gate_weights(w_r, w_i):
    per = GATE_SUPER // BLK_W

    def block_diag(w):
        w4 = w.reshape(N_SUPER, per, BLK_W, BLK_W)
        eye = jnp.eye(per, dtype=w.dtype)
        return jnp.einsum('saij,ac->saicj', w4, eye).reshape(N_SUPER, GATE_SUPER, GATE_SUPER)

    dr, di = block_diag(w_r), block_diag(w_i)
    out = []
    for j, (lo, hi) in enumerate(GATE_K_RANGES):
        cols = slice(j * LANES, (j + 1) * LANES)
        out.append(jnp.concatenate([dr[:, lo:hi, cols], di[:, lo:hi, cols]], axis=-1).astype(BF16))
    return out


def kernel(x_prompt, x_sample, state_lru_h, state_lru_conv, state_hgrn, norm_gain, a_w_in, a_conv_w, a_conv_b,
           a_w_r, a_b_r, a_w_i, a_b_i, a_lambda, a_w_out, b_w_in, b_lb_logits, b_o_gain, b_w_out, final_gain):
    assert norm_gain.shape[0] == 2 and a_w_in.shape[0] == 1 and b_w_in.shape[0] == 1
    pb, pt, _ = x_prompt.shape
    sb, st, _ = x_sample.shape
    halo_t = CONV_W - 1

    row = lambda p: p.reshape(1, -1)
    lru_params = (row(norm_gain[0]), a_w_in[0].astype(BF16), a_conv_w[0], row(a_conv_b[0]),
                  *_gate_weights(a_w_r[0], a_w_i[0]), row(a_b_r[0]), row(a_b_i[0]), row(a_lambda[0]),
                  a_w_out[0].astype(BF16))
    hgrn_params = (row(norm_gain[1]), b_w_in[0].astype(BF16), b_lb_logits, row(b_o_gain[0]),
                   b_w_out[0].astype(BF16), row(final_gain))

    x1p, hp, bufp = _lru_layer(x_prompt, jnp.zeros((pb, W_LRU), F32), jnp.zeros((halo_t * pb, W_LRU), F32),
                               *lru_params, nb=pb, tt=64, batch_major=True)
    y_prompt, sp = _hgrn_prompt(x1p, *hgrn_params, tt=512)
    bufp = bufp.reshape(halo_t, pb, W_LRU).transpose(1, 0, 2)

    xs = x_sample.transpose(1, 0, 2).reshape(st * sb, D_MODEL)
    bufs0 = state_lru_conv[0].transpose(1, 0, 2).reshape(halo_t * sb, W_LRU)
    x1s, hs, bufs = _lru_layer(xs, state_lru_h[0], bufs0, *lru_params, nb=sb, tt=st, batch_major=False)
    ys, ss = _hgrn_decode(x1s, state_hgrn[0], *hgrn_params, nb=sb, tt=st, bstep=4)
    bufs = bufs.reshape(halo_t, sb, W_LRU).transpose(1, 0, 2)

    return (y_prompt, ys.reshape(sb, st, D_MODEL), hp[None], bufp[None], sp[None],
            hs[None], bufs[None], ss[None])
```

```python
import functools

import jax
import jax.numpy as jnp
from jax import lax
from jax.experimental import pallas as pl
from jax.experimental.pallas import tpu as pltpu

F32 = jnp.float32
BF16 = jnp.bfloat16

D_MODEL = 1024
W_LRU = 1536
N_BLK = 16
BLK_W = W_LRU // N_BLK
CONV_W = 4
LRU_C = 8.0
H_B = 8
DK_B = 128
DV_B = 128
INNER_B = H_B * DK_B
EPS = 1e-6

LANES = 128
GATE_SUPER = 384
N_SUPER = W_LRU // GATE_SUPER
LRU_PARTS = 2
HGRN_PARTS = 2
CHUNK = 64
LOG_CHUNK = 6
VMEM_LIMIT = 56 * 1024 * 1024
MAX_LOCAL_EXPONENT = 75.0
LOG_FLOOR = -1e4


def _silu_from_half(hx):
    return hx + hx * jnp.tanh(hx)


def _rms_scale(x, gain):
    ms = jnp.mean(x * x, axis=-1, keepdims=True)
    return x * lax.rsqrt(ms + EPS) * gain


def _const_spec(shape):
    zeros = (0,) * len(shape)
    return pl.BlockSpec(shape, lambda *_: zeros, pipeline_mode=pl.Buffered(1))


def _gate_k_ranges():
    ranges = []
    for j in range(GATE_SUPER // LANES):
        first_blk = (j * LANES) // BLK_W
        last_blk = ((j + 1) * LANES - 1) // BLK_W
        lo = (first_blk * BLK_W) // LANES * LANES
        hi = -(-((last_blk + 1) * BLK_W) // LANES) * LANES
        ranges.append((lo, hi))
    return ranges


GATE_K_RANGES = _gate_k_ranges()


def _lru_kernel(nb, tt, batch_major, x_ref, gain_ref, win_ref, cw_ref, cb_ref, wg0_ref, wg1_ref, wg2_ref,
                br_ref, bi_ref, lam_ref, wout_ref, h0_ref, buf0_ref, x1_ref, hlast_ref, bufout_ref, *scratch):
    wg_refs = (wg0_ref, wg1_ref, wg2_ref)
    if batch_major:
        xs_s, os_s, xbuf_s, a_s, b_s, g_s, hc_s = scratch
    else:
        xbuf_s, a_s, b_s, g_s, hc_s = scratch
    rows = nb * tt
    halo = (CONV_W - 1) * nb
    n_lane_tiles = D_MODEL // LANES
    tp = tt // LRU_PARTS
    rp = nb * tp

    @pl.when(pl.program_id(0) == 0)
    def _():
        hc_s[...] = h0_ref[...]
        xbuf_s[0:halo, :] = buf0_ref[...]

    if batch_major:
        for b in range(nb):
            for j in range(n_lane_tiles):
                xs_s[j, pl.ds(b, tt, stride=nb), :] = x_ref[b, :, j * LANES:(j + 1) * LANES]

    def load_x(p):
        if batch_major:
            return jnp.concatenate([xs_s[j, p * rp:(p + 1) * rp, :] for j in range(n_lane_tiles)], axis=-1)
        return x_ref[p * rp:(p + 1) * rp, :]

    nlam = -lam_ref[...]
    softplus = jnp.maximum(nlam, 0.0) + jnp.log1p(jnp.exp(-jnp.abs(nlam)))
    half_rate = (-0.5 * LRU_C) * softplus

    def pre(p):
        r0 = p * rp
        xn = _rms_scale(load_x(p), gain_ref[...]).astype(BF16)
        u = jnp.dot(xn, win_ref[...], preferred_element_type=F32)
        g_s[r0:r0 + rp, :] = _silu_from_half(u[:, W_LRU:])
        xbuf_s[halo + r0:halo + r0 + rp, :] = u[:, :W_LRU]
        xh = cb_ref[...] + cw_ref[0:1, :] * xbuf_s[r0:r0 + rp, :]
        for k in range(1, CONV_W):
            xh = xh + cw_ref[k:k + 1, :] * xbuf_s[r0 + k * nb:r0 + k * nb + rp, :]
        xhb = xh.astype(BF16)
        r_parts, i_parts = [], []
        for s in range(N_SUPER):
            for (lo, hi), wg_ref in zip(GATE_K_RANGES, wg_refs):
                gsi = jnp.dot(xhb[:, s * GATE_SUPER + lo:s * GATE_SUPER + hi], wg_ref[s],
                              preferred_element_type=F32)
                r_parts.append(gsi[:, :LANES])
                i_parts.append(gsi[:, LANES:])
        tr = jnp.tanh(jnp.concatenate(r_parts, axis=-1) + br_ref[...])
        ti = jnp.tanh(jnp.concatenate(i_parts, axis=-1) + bi_ref[...])
        log_a = half_rate + half_rate * tr
        a = jnp.exp(log_a)
        one_m_a2 = -jnp.tanh(log_a) * (a * a + 1.0)
        a_s[r0:r0 + rp, :] = a
        root = jnp.where(one_m_a2 > 0.0, one_m_a2 * lax.rsqrt(one_m_a2), 0.0)
        b_s[r0:r0 + rp, :] = root * (xh * (1.0 + ti))

    def scan(p, h):
        for t in range(p * tp, (p + 1) * tp):
            h = a_s[t * nb:(t + 1) * nb, :] * h + b_s[t * nb:(t + 1) * nb, :]
            b_s[t * nb:(t + 1) * nb, :] = h
        return h

    def post(p):
        r0 = p * rp
        y = (b_s[r0:r0 + rp, :] * g_s[r0:r0 + rp, :]).astype(BF16)
        out = load_x(p) + jnp.dot(y, wout_ref[...], preferred_element_type=F32)
        if batch_major:
            for j in range(n_lane_tiles):
                os_s[j, r0:r0 + rp, :] = out[:, j * LANES:(j + 1) * LANES]
            for b in range(nb):
                x1_ref[b, p * tp:(p + 1) * tp, :] = jnp.concatenate(
                    [os_s[j, pl.ds(r0 + b, tp, stride=nb), :] for j in range(n_lane_tiles)], axis=-1)
        else:
            x1_ref[r0:r0 + rp, :] = out

    for p in range(LRU_PARTS):
        pre(p)
    new_halo = xbuf_s[rows:rows + halo, :]
    bufout_ref[...] = new_halo
    xbuf_s[0:halo, :] = new_halo
    h = hc_s[...]
    for p in range(LRU_PARTS):
        h = scan(p, h)
        post(p)
    hc_s[...] = h
    hlast_ref[...] = h


def _lru_layer(x, h0, buf0, gain, w_in, conv_w, conv_b, wg0, wg1, wg2, b_r, b_i, lam, w_out, *, nb, tt, batch_major):
    if batch_major:
        total_t = x.shape[1]
        x_spec = pl.BlockSpec((nb, tt, D_MODEL), lambda i: (0, i, 0))
        x_shape = jax.ShapeDtypeStruct((nb, total_t, D_MODEL), F32)
    else:
        total_t = x.shape[0] // nb
        x_spec = pl.BlockSpec((nb * tt, D_MODEL), lambda i: (i, 0))
        x_shape = jax.ShapeDtypeStruct((nb * total_t, D_MODEL), F32)
    rows = nb * tt
    halo = (CONV_W - 1) * nb
    scratch = []
    if batch_major:
        scratch += [pltpu.VMEM((D_MODEL // LANES, rows, LANES), F32),
                    pltpu.VMEM((D_MODEL // LANES, rows, LANES), F32)]
    scratch += [pltpu.VMEM((rows + halo, W_LRU), F32),
                pltpu.VMEM((rows, W_LRU), F32),
                pltpu.VMEM((rows, W_LRU), F32),
                pltpu.VMEM((rows, W_LRU), F32),
                pltpu.VMEM((nb, W_LRU), F32)]
    return pl.pallas_call(
        functools.partial(_lru_kernel, nb, tt, batch_major),
        grid=(total_t // tt,),
        in_specs=[x_spec,
                  _const_spec((1, D_MODEL)),
                  _const_spec((D_MODEL, 2 * W_LRU)),
                  _const_spec((CONV_W, W_LRU)),
                  _const_spec((1, W_LRU)),
                  *[_const_spec((N_SUPER, hi - lo, 2 * LANES)) for lo, hi in GATE_K_RANGES],
                  _const_spec((1, W_LRU)),
                  _const_spec((1, W_LRU)),
                  _const_spec((1, W_LRU)),
                  _const_spec((W_LRU, D_MODEL)),
                  _const_spec((nb, W_LRU)),
                  _const_spec((halo, W_LRU))],
        out_specs=[x_spec,
                   pl.BlockSpec((nb, W_LRU), lambda i: (0, 0)),
                   pl.BlockSpec((halo, W_LRU), lambda i: (0, 0))],
        out_shape=[x_shape,
                   jax.ShapeDtypeStruct((nb, W_LRU), F32),
                   jax.ShapeDtypeStruct((halo, W_LRU), F32)],
        scratch_shapes=scratch,
        compiler_params=pltpu.CompilerParams(dimension_semantics=("arbitrary",),
                                             vmem_limit_bytes=VMEM_LIMIT),
        name="lru_layer_bm" if batch_major else "lru_layer_tm",
    )(x, gain, w_in, conv_w, conv_b, wg0, wg1, wg2, b_r, b_i, lam, w_out, h0, buf0)


def _lower_bound(lbl_ref):
    l0 = lbl_ref[0:1, :]
    l1 = lbl_ref[1:2, :]
    mx = jnp.maximum(l0, l1)
    e0 = jnp.exp(l0 - mx)
    e1 = jnp.exp(l1 - mx)
    return e1 / (e0 + e1)


def _hgrn_front(x, gain_ref, win_ref, lbl_ref):
    xn = _rms_scale(x, gain_ref[...]).astype(BF16)
    n = INNER_B
    q = _silu_from_half(jnp.dot(xn, win_ref[:, 0:n], preferred_element_type=F32))
    tf = jnp.tanh(jnp.dot(xn, win_ref[:, n:2 * n], preferred_element_type=F32))
    v = jnp.dot(xn, win_ref[:, 2 * n:3 * n], preferred_element_type=F32)
    gs = _silu_from_half(jnp.dot(xn, win_ref[:, 3 * n:4 * n], preferred_element_type=F32))
    lb = _lower_bound(lbl_ref)
    c1 = 0.5 * (1.0 - lb)
    ct = c1 * tf
    kk = c1 - ct
    g = (lb + c1) + ct
    lg = jnp.maximum(jnp.log(g), LOG_FLOOR)
    return q, kk, v, lg, gs


def _hgrn_back(x, o, gs, og_ref, wout_ref, fg_ref):
    parts = []
    for h in range(H_B):
        oh = o[:, h * DV_B:(h + 1) * DV_B]
        ms = jnp.mean(oh * oh, axis=-1, keepdims=True)
        parts.append(oh * lax.rsqrt(ms + EPS))
    on = jnp.concatenate(parts, axis=-1) * og_ref[...]
    y = (on * gs).astype(BF16)
    x2 = x + jnp.dot(y, wout_ref[...], preferred_element_type=F32)
    return _rms_scale(x2, fg_ref[...])


def _nt(a, b):
    return lax.dot_general(a, b, (((1,), (1,)), ((), ())), preferred_element_type=F32)


def _tn(a, b):
    return lax.dot_general(a, b, (((0,), (0,)), ((), ())), preferred_element_type=F32)


def _hgrn_prompt_kernel(tt, x_ref, gain_ref, win_ref, lbl_ref, og_ref, wout_ref, fg_ref,
                        y_ref, st_ref, q_s, k_s, v_s, cum_s, gs_s, o_s, st_s, st0_s):
    i = pl.program_id(1)
    n_chunks = tt // CHUNK
    half = CHUNK // 2
    rp = tt // HGRN_PARTS
    chunks_per_part = rp // CHUNK

    @pl.when(i == 0)
    def _():
        st_s[...] = jnp.zeros_like(st_s)

    st0_s[...] = st_s[...]

    row_c = lax.broadcasted_iota(jnp.int32, (CHUNK, CHUNK), 0)
    col_c = lax.broadcasted_iota(jnp.int32, (CHUNK, CHUNK), 1)
    tri = (row_c >= col_c).astype(BF16)
    tri3 = jnp.concatenate([tri, tri, tri], axis=-1)

    def front(p):
        r0 = p * rp
        q, kk, v, lg, gs = _hgrn_front(x_ref[r0:r0 + rp, :], gain_ref, win_ref, lbl_ref)
        q_s[r0:r0 + rp, :] = q
        k_s[r0:r0 + rp, :] = kk
        v_s[r0:r0 + rp, :] = v.astype(BF16)
        gs_s[r0:r0 + rp, :] = gs
        h1 = lg.astype(BF16)
        r1 = lg - h1.astype(F32)
        h2 = r1.astype(BF16)
        h3 = (r1 - h2.astype(F32)).astype(BF16)
        worst = jnp.zeros((1, INNER_B), F32)
        for c in range(chunks_per_part):
            rs = slice(c * CHUNK, (c + 1) * CHUNK)
            cum = jnp.dot(tri3, jnp.concatenate([h1[rs], h2[rs], h3[rs]], axis=0), preferred_element_type=F32)
            cum_s[r0 + c * CHUNK:r0 + (c + 1) * CHUNK, :] = cum
            mid = cum[half - 1:half, :]
            worst = jnp.maximum(worst, jnp.maximum(-mid, mid - cum[CHUNK - 1:CHUNK, :]))
        return worst

    def back(p):
        r0 = p * rp
        y_ref[r0:r0 + rp, :] = _hgrn_back(x_ref[r0:r0 + rp, :], o_s[r0:r0 + rp, :], gs_s[r0:r0 + rp, :],
                                           og_ref, wout_ref, fg_ref)

    def head_update(h, base, att, qe_b, ke_b, d):
        sl = slice(h * DK_B, (h + 1) * DK_B)
        vh = v_s[pl.ds(base, CHUNK), sl]
        st = st_s[h]
        o = _nt(qe_b[:, sl], st.astype(BF16)) + jnp.dot(att.astype(BF16), vh, preferred_element_type=F32)
        st_s[h] = d[:, sl] * st + _tn(vh, ke_b[:, sl])
        o_s[pl.ds(base, CHUNK), sl] = o

    def chunk_common(base):
        cu = cum_s[pl.ds(base, CHUNK), :]
        tail = cum_s[pl.ds(base + CHUNK - 8, 8), :]
        clast = tail[7:8, :]
        qf = q_s[pl.ds(base, CHUNK), :]
        kf = k_s[pl.ds(base, CHUNK), :]
        qe = qf * jnp.exp(cu)
        ke_b = (kf * jnp.exp(clast - cu)).astype(BF16)
        return cu, qf, kf, qe, ke_b, jnp.exp(clast)

    def fast_chunk(base):
        cu, qf, kf, qe, ke_b, d = chunk_common(base)
        cmid = cum_s[pl.ds(base + half - 8, 8), :][7:8, :]
        right = lax.broadcasted_iota(jnp.int32, (CHUNK, INNER_B), 0) >= half
        da = cu - cmid
        za = jnp.where(right, qf, kf) * jnp.exp(-jnp.abs(da))
        kl_b = (kf * jnp.exp(jnp.where(right, -da, -cu))).astype(BF16)
        ql_b = jnp.where(right, za, qe).astype(BF16)
        za_b = za.astype(BF16)
        qe_b = qe.astype(BF16)
        cross = (row_c >= half) & (col_c < half)
        local = (row_c >= col_c) & ((row_c >= half) == (col_c >= half))
        for h in range(H_B):
            sl = slice(h * DK_B, (h + 1) * DK_B)
            pa = _nt(za_b[:, sl], za_b[:, sl])
            pb = _nt(ql_b[:, sl], kl_b[:, sl])
            att = jnp.where(cross, pa, jnp.where(local, pb, 0.0))
            head_update(h, base, att, qe_b, ke_b, d)

    def slow_chunk(c, carry):
        base = pl.multiple_of(c * CHUNK, CHUNK)
        cu_all, qf, kf, qe, ke_b, d = chunk_common(base)
        qe_b = qe.astype(BF16)
        xor_c = row_c ^ col_c
        row_l = lax.broadcasted_iota(jnp.int32, (CHUNK, LANES), 0)
        sub8 = lax.broadcasted_iota(jnp.int32, (8, LANES), 0)
        for h in range(H_B):
            sl = slice(h * DK_B, (h + 1) * DK_B)
            cu = cu_all[:, sl]
            qh = qf[:, sl]
            kh = kf[:, sl]

            def ref_row(r):
                grp = cu[(r // 8) * 8:(r // 8) * 8 + 8, :]
                return grp[r % 8:r % 8 + 1, :]

            att = jnp.zeros((CHUNK, CHUNK), F32)
            for lm in range(LOG_CHUNK):
                m = 1 << lm
                if lm >= 2:
                    cm = jnp.concatenate(
                        [jnp.broadcast_to(ref_row(blk * 2 * m + m - 1), (2 * m, LANES))
                         for blk in range(CHUNK // (2 * m))], axis=0)
                elif lm == 1:
                    cm = jnp.concatenate(
                        [jnp.where(sub8 < 4,
                                   jnp.broadcast_to(ref_row(blk * 8 + 1), (8, LANES)),
                                   jnp.broadcast_to(ref_row(blk * 8 + 5), (8, LANES)))
                         for blk in range(CHUNK // 8)], axis=0)
                else:
                    cm = jnp.where((row_l & 1) == 1, pltpu.roll(cu, 1, axis=0), cu)
                right = ((row_l >> lm) & 1) == 1
                z = (jnp.where(right, qh, kh) * jnp.exp(-jnp.abs(cu - cm))).astype(BF16)
                pair = ((xor_c >> lm) == 1) & (((row_c >> lm) & 1) == 1)
                att = jnp.where(pair, _nt(z, z), att)
            att = jnp.where(row_c == col_c, _nt(qh.astype(BF16), kh.astype(BF16)), att)
            head_update(h, base, att, qe_b, ke_b, d)
        return carry

    worst = front(0)
    for p in range(1, HGRN_PARTS):
        worst = jnp.maximum(worst, front(p))
    for c in range(n_chunks):
        fast_chunk(c * CHUNK)
    for p in range(HGRN_PARTS):
        back(p)

    @pl.when(jnp.max(worst) > MAX_LOCAL_EXPONENT)
    def _():
        st_s[...] = st0_s[...]
        lax.fori_loop(0, n_chunks, slow_chunk, 0)
        for p in range(HGRN_PARTS):
            back(p)

    @pl.when(i == pl.num_programs(1) - 1)
    def _():
        for h in range(H_B):
            st_ref[h] = st_s[h].T


def _hgrn_prompt(x1, gain, w_in, lbl, o_gain, w_out, f_gain, *, tt):
    nb, total_t, _ = x1.shape
    return pl.pallas_call(
        functools.partial(_hgrn_prompt_kernel, tt),
        grid=(nb, total_t // tt),
        in_specs=[pl.BlockSpec((None, tt, D_MODEL), lambda b, i: (b, i, 0)),
                  _const_spec((1, D_MODEL)),
                  _const_spec((D_MODEL, 4 * INNER_B)),
                  _const_spec((2, INNER_B)),
                  _const_spec((1, INNER_B)),
                  _const_spec((INNER_B, D_MODEL)),
                  _const_spec((1, D_MODEL))],
        out_specs=[pl.BlockSpec((None, tt, D_MODEL), lambda b, i: (b, i, 0)),
                   pl.BlockSpec((None, H_B, DK_B, DV_B), lambda b, i: (b, 0, 0, 0))],
        out_shape=[jax.ShapeDtypeStruct((nb, total_t, D_MODEL), F32),
                   jax.ShapeDtypeStruct((nb, H_B, DK_B, DV_B), F32)],
        scratch_shapes=[pltpu.VMEM((tt, INNER_B), F32),
                        pltpu.VMEM((tt, INNER_B), F32),
                        pltpu.VMEM((tt, INNER_B), BF16),
                        pltpu.VMEM((tt, INNER_B), F32),
                        pltpu.VMEM((tt, INNER_B), F32),
                        pltpu.VMEM((tt, INNER_B), F32),
                        pltpu.VMEM((H_B, DV_B, DK_B), F32),
                        pltpu.VMEM((H_B, DV_B, DK_B), F32)],
        compiler_params=pltpu.CompilerParams(dimension_semantics=("arbitrary", "arbitrary"),
                                             vmem_limit_bytes=VMEM_LIMIT),
        name="hgrn_prompt",
    )(x1, gain, w_in, lbl, o_gain, w_out, f_gain)


def _hgrn_decode_kernel(nb, tt, bstep, x_ref, gain_ref, win_ref, lbl_ref, og_ref, wout_ref, fg_ref, s0_ref,
                        y_ref, snew_ref, qe_s, ke_s, dd_s, v_s, oi_s, gs_s, x_s, o_s):
    j = pl.program_id(0)
    rows = nb * tt

    def to_batch_major(dst, val, t):
        for h in range(H_B):
            dst[h, pl.ds(t, nb, stride=tt), :] = val[:, h * LANES:(h + 1) * LANES]

    @pl.when(j == 0)
    def _():
        x = x_ref[...]
        q, kk, v, lg, gs = _hgrn_front(x, gain_ref, win_ref, lbl_ref)
        ones_blk = jnp.ones((DK_B, DK_B), BF16)

        def slab(arr, t):
            return arr[t * nb:(t + 1) * nb, :]

        cum = [slab(lg, 0)]
        for t in range(1, tt):
            cum.append(cum[-1] + slab(lg, t))
        dd_s[...] = jnp.zeros_like(dd_s)
        d = jnp.exp(cum[tt - 1])
        d_hi = d.astype(BF16).astype(F32)
        to_batch_major(dd_s, d_hi, 0)
        to_batch_major(dd_s, d - d_hi, 1)
        for t in range(tt):
            qt = slab(q, t)
            acc = jnp.zeros((nb, INNER_B), F32)
            for s in range(t + 1):
                prod = qt * slab(kk, s)
                if s < t:
                    prod = prod * jnp.exp(cum[t] - cum[s])
                pb = prod.astype(BF16)
                att = jnp.concatenate(
                    [jnp.dot(pb[:, h * DK_B:(h + 1) * DK_B], ones_blk, preferred_element_type=F32)
                     for h in range(H_B)], axis=-1)
                acc = acc + att * slab(v, s)
            to_batch_major(oi_s, acc, t)
            to_batch_major(qe_s, qt * jnp.exp(cum[t]), t)
            to_batch_major(ke_s, slab(kk, t) * jnp.exp(cum[tt - 1] - cum[t]), t)
            to_batch_major(v_s, slab(v, t), t)
            to_batch_major(gs_s, slab(gs, t), t)
            to_batch_major(x_s, slab(x, t), t)

    row1 = lax.broadcasted_iota(jnp.int32, (8, LANES), 0)
    lane2 = lax.broadcasted_iota(jnp.int32, (8, 2 * DV_B), 1)
    row2 = lax.broadcasted_iota(jnp.int32, (8, 2 * DV_B), 0)
    own = (row1 < tt, row1 >= tt)
    ones_rows = tuple(
        jnp.where((lane2 >= DV_B) & (row2 >= lo) & (row2 < lo + 2), 1.0, 0.0).astype(BF16) for lo in (tt, 0))

    def pair_body(pi, carry):
        off = pl.multiple_of((j * (bstep // 2) + pi) * 8, 8)
        for h in range(H_B):
            qe = qe_s[h, pl.ds(off, 8), :].astype(BF16)
            ke = ke_s[h, pl.ds(off, 8), :]
            dd = pltpu.roll(dd_s[h, pl.ds(off, 8), :], tt, axis=0)
            vv = v_s[h, pl.ds(off, 8), :]
            inter = []
            for e in range(2):
                s0 = s0_ref[2 * pi + e, h]
                inter.append(jnp.dot(qe, s0.astype(BF16), preferred_element_type=F32))
                lhs = jnp.where(own[e], ke, dd).astype(BF16)
                ve = jnp.where(own[e], vv, 0.0).astype(BF16)
                rhs = jnp.concatenate([ve, jnp.zeros((8, DV_B), BF16)], axis=-1) + ones_rows[e]
                upd = lax.dot_general(lhs, rhs, (((0,), (0,)), ((), ())),
                                      preferred_element_type=F32)
                snew_ref[2 * pi + e, h] = upd[:, DV_B:] * s0 + upd[:, :DV_B]
            o_s[pl.ds(off, 8), h * DV_B:(h + 1) * DV_B] = (
                jnp.where(own[0], inter[0], inter[1]) + oi_s[h, pl.ds(off, 8), :])
        return carry

    lax.fori_loop(0, bstep // 2, pair_body, 0)

    @pl.when(j == pl.num_programs(0) - 1)
    def _():
        xb = jnp.concatenate([x_s[h] for h in range(H_B)], axis=-1)
        gsb = jnp.concatenate([gs_s[h] for h in range(H_B)], axis=-1)
        y_ref[...] = _hgrn_back(xb, o_s[...], gsb, og_ref, wout_ref, fg_ref)


def _hgrn_decode(x1, s0, gain, w_in, lbl, o_gain, w_out, f_gain, *, nb, tt, bstep):
    assert 2 * tt == 8 and bstep % 2 == 0
    rows = nb * tt
    slab = pltpu.VMEM((H_B, rows, LANES), F32)
    return pl.pallas_call(
        functools.partial(_hgrn_decode_kernel, nb, tt, bstep),
        grid=(nb // bstep,),
        in_specs=[_const_spec((rows, D_MODEL)),
                  _const_spec((1, D_MODEL)),
                  _const_spec((D_MODEL, 4 * INNER_B)),
                  _const_spec((2, INNER_B)),
                  _const_spec((1, INNER_B)),
                  _const_spec((INNER_B, D_MODEL)),
                  _const_spec((1, D_MODEL)),
                  pl.BlockSpec((bstep, H_B, DK_B, DV_B), lambda j: (j, 0, 0, 0))],
        out_specs=[pl.BlockSpec((rows, D_MODEL), lambda j: (0, 0)),
                   pl.BlockSpec((bstep, H_B, DK_B, DV_B), lambda j: (j, 0, 0, 0))],
        out_shape=[jax.ShapeDtypeStruct((rows, D_MODEL), F32),
                   jax.ShapeDtypeStruct((nb, H_B, DK_B, DV_B), F32)],
        scratch_shapes=[slab, slab, slab, slab, slab, slab, slab,
                        pltpu.VMEM((rows, INNER_B), F32)],
        compiler_params=pltpu.CompilerParams(dimension_semantics=("arbitrary",),
                                             vmem_limit_bytes=VMEM_LIMIT),
        name="hgrn_decode",
    )(x1, gain, w_in, lbl, o_gain, w_out, f_gain, s0)


def _gate_weights(w_r, w_i):
    per = GATE_SUPER // BLK_W

    def block_diag(w):
        w4 = w.reshape(N_SUPER, per, BLK_W, BLK_W)
        eye = jnp.eye(per, dtype=w.dtype)
        return jnp.einsum('saij,ac->saicj', w4, eye).reshape(N_SUPER, GATE_SUPER, GATE_SUPER)

    dr, di = block_diag(w_r), block_diag(w_i)
    out = []
    for j, (lo, hi) in enumerate(GATE_K_RANGES):
        cols = slice(j * LANES, (j + 1) * LANES)
        out.append(jnp.concatenate([dr[:, lo:hi, cols], di[:, lo:hi, cols]], axis=-1).astype(BF16))
    return out


def kernel(x_prompt, x_sample, state_lru_h, state_lru_conv, state_hgrn, norm_gain, a_w_in, a_conv_w, a_conv_b,
           a_w_r, a_b_r, a_w_i, a_b_i, a_lambda, a_w_out, b_w_in, b_lb_logits, b_o_gain, b_w_out, final_gain):
    assert norm_gain.shape[0] == 2 and a_w_in.shape[0] == 1 and b_w_in.shape[0] == 1
    pb, pt, _ = x_prompt.shape
    sb, st, _ = x_sample.shape
    halo_t = CONV_W - 1

    row = lambda p: p.reshape(1, -1)
    lru_col_scale = jnp.concatenate([jnp.ones((W_LRU,), F32), jnp.full((W_LRU,), 0.5, F32)])
    hgrn_col_scale = jnp.concatenate([jnp.full((2 * INNER_B,), 0.5, F32), jnp.ones((INNER_B,), F32),
                                      jnp.full((INNER_B,), 0.5, F32)])
    lru_params = (row(norm_gain[0]), (a_w_in[0] * lru_col_scale).astype(BF16), 0.5 * a_conv_w[0],
                  row(0.5 * a_conv_b[0]), *_gate_weights(a_w_r[0], a_w_i[0]), row(0.5 * a_b_r[0]),
                  row(0.5 * a_b_i[0]), row(a_lambda[0]), a_w_out[0].astype(BF16))
    hgrn_params = (row(norm_gain[1]), (b_w_in[0] * hgrn_col_scale).astype(BF16), b_lb_logits, row(b_o_gain[0]),
                   b_w_out[0].astype(BF16), row(final_gain))

    x1p, hp, bufp = _lru_layer(x_prompt, jnp.zeros((pb, W_LRU), F32), jnp.zeros((halo_t * pb, W_LRU), F32),
                               *lru_params, nb=pb, tt=64, batch_major=True)
    y_prompt, sp = _hgrn_prompt(x1p, *hgrn_params, tt=512)
    bufp = bufp.reshape(halo_t, pb, W_LRU).transpose(1, 0, 2)

    xs = x_sample.transpose(1, 0, 2).reshape(st * sb, D_MODEL)
    bufs0 = state_lru_conv[0].transpose(1, 0, 2).reshape(halo_t * sb, W_LRU)
    x1s, hs, bufs = _lru_layer(xs, state_lru_h[0], bufs0, *lru_params, nb=sb, tt=st, batch_major=False)
    ys, ss = _hgrn_decode(x1s, state_hgrn[0], *hgrn_params, nb=sb, tt=st, bstep=4)
    bufs = bufs.reshape(halo_t, sb, W_LRU).transpose(1, 0, 2)

    return (y_prompt, ys.reshape(sb, st, D_MODEL), hp[None], bufp[None], sp[None],
            hs[None], bufs[None], ss[None])
```

```python
import functools

import jax
import jax.numpy as jnp
from jax import lax
from jax.experimental import pallas as pl
from jax.experimental.pallas import tpu as pltpu

F32 = jnp.float32
BF16 = jnp.bfloat16

D_MODEL = 1024
W_LRU = 1536
N_BLK = 16
BLK_W = W_LRU // N_BLK
CONV_W = 4
LRU_C = 8.0
H_B = 8
DK_B = 128
DV_B = 128
INNER_B = H_B * DK_B
EPS = 1e-6

LANES = 128
GATE_SUPER = 384
N_SUPER = W_LRU // GATE_SUPER
LRU_PARTS = 2
HGRN_PARTS = 2
CHUNK = 64
LOG_CHUNK = 6
VMEM_LIMIT = 56 * 1024 * 1024
MAX_LOCAL_EXPONENT = 75.0
LOG2_E = 1.4426950408889634
LOG_FLOOR = -1e4


def _silu_from_half(hx):
    return hx + hx * jnp.tanh(hx)


def _rms_scale(x, gain):
    ms = jnp.mean(x * x, axis=-1, keepdims=True)
    return x * lax.rsqrt(ms + EPS) * gain


def _const_spec(shape):
    zeros = (0,) * len(shape)
    return pl.BlockSpec(shape, lambda *_: zeros, pipeline_mode=pl.Buffered(1))


def _gate_k_ranges():
    ranges = []
    for j in range(GATE_SUPER // LANES):
        first_blk = (j * LANES) // BLK_W
        last_blk = ((j + 1) * LANES - 1) // BLK_W
        lo = (first_blk * BLK_W) // LANES * LANES
        hi = -(-((last_blk + 1) * BLK_W) // LANES) * LANES
        ranges.append((lo, hi))
    return ranges


GATE_K_RANGES = _gate_k_ranges()


def _lru_kernel(nb, tt, batch_major, x_ref, gain_ref, win_ref, cw_ref, cb_ref, wg0_ref, wg1_ref, wg2_ref,
                br_ref, bi_ref, lam_ref, wout_ref, h0_ref, buf0_ref, x1_ref, hlast_ref, bufout_ref, *scratch):
    wg_refs = (wg0_ref, wg1_ref, wg2_ref)
    if batch_major:
        xs_s, os_s, xbuf_s, a_s, b_s, g_s, hc_s = scratch
    else:
        xbuf_s, a_s, b_s, g_s, hc_s = scratch
    rows = nb * tt
    halo = (CONV_W - 1) * nb
    n_lane_tiles = D_MODEL // LANES
    tp = tt // LRU_PARTS
    rp = nb * tp

    @pl.when(pl.program_id(0) == 0)
    def _():
        hc_s[...] = h0_ref[...]
        xbuf_s[0:halo, :] = buf0_ref[...]

    if batch_major:
        for b in range(nb):
            for j in range(n_lane_tiles):
                xs_s[j, pl.ds(b, tt, stride=nb), :] = x_ref[b, :, j * LANES:(j + 1) * LANES]

    def load_x(p):
        if batch_major:
            return jnp.concatenate([xs_s[j, p * rp:(p + 1) * rp, :] for j in range(n_lane_tiles)], axis=-1)
        return x_ref[p * rp:(p + 1) * rp, :]

    nlam = -lam_ref[...]
    softplus = jnp.maximum(nlam, 0.0) + jnp.log1p(jnp.exp(-jnp.abs(nlam)))
    half_rate = (-0.5 * LRU_C) * softplus

    def pre(p):
        r0 = p * rp
        xn = _rms_scale(load_x(p), gain_ref[...]).astype(BF16)
        u = jnp.dot(xn, win_ref[...], preferred_element_type=F32)
        g_s[r0:r0 + rp, :] = _silu_from_half(u[:, W_LRU:])
        xbuf_s[halo + r0:halo + r0 + rp, :] = u[:, :W_LRU]
        xh = cb_ref[...] + cw_ref[0:1, :] * xbuf_s[r0:r0 + rp, :]
        for k in range(1, CONV_W):
            xh = xh + cw_ref[k:k + 1, :] * xbuf_s[r0 + k * nb:r0 + k * nb + rp, :]
        xhb = xh.astype(BF16)
        r_parts, i_parts = [], []
        for s in range(N_SUPER):
            for (lo, hi), wg_ref in zip(GATE_K_RANGES, wg_refs):
                gsi = jnp.dot(xhb[:, s * GATE_SUPER + lo:s * GATE_SUPER + hi], wg_ref[s],
                              preferred_element_type=F32)
                r_parts.append(gsi[:, :LANES])
                i_parts.append(gsi[:, LANES:])
        tr = jnp.tanh(jnp.concatenate(r_parts, axis=-1) + br_ref[...])
        ti = jnp.tanh(jnp.concatenate(i_parts, axis=-1) + bi_ref[...])
        log_a = half_rate + half_rate * tr
        a = jnp.exp(log_a)
        one_m_a2 = -jnp.tanh(log_a) * (a * a + 1.0)
        a_s[r0:r0 + rp, :] = a
        root = jnp.where(one_m_a2 > 0.0, one_m_a2 * lax.rsqrt(one_m_a2), 0.0)
        b_s[r0:r0 + rp, :] = root * (xh * (1.0 + ti))

    def scan(p, h):
        for t in range(p * tp, (p + 1) * tp):
            h = a_s[t * nb:(t + 1) * nb, :] * h + b_s[t * nb:(t + 1) * nb, :]
            b_s[t * nb:(t + 1) * nb, :] = h
        return h

    def post(p):
        r0 = p * rp
        y = (b_s[r0:r0 + rp, :] * g_s[r0:r0 + rp, :]).astype(BF16)
        out = load_x(p) + jnp.dot(y, wout_ref[...], preferred_element_type=F32)
        if batch_major:
            for j in range(n_lane_tiles):
                os_s[j, r0:r0 + rp, :] = out[:, j * LANES:(j + 1) * LANES]
            for b in range(nb):
                x1_ref[b, p * tp:(p + 1) * tp, :] = jnp.concatenate(
                    [os_s[j, pl.ds(r0 + b, tp, stride=nb), :] for j in range(n_lane_tiles)], axis=-1)
        else:
            x1_ref[r0:r0 + rp, :] = out

    for p in range(LRU_PARTS):
        pre(p)
    new_halo = xbuf_s[rows:rows + halo, :]
    bufout_ref[...] = new_halo
    xbuf_s[0:halo, :] = new_halo
    h = hc_s[...]
    for p in range(LRU_PARTS):
        h = scan(p, h)
        post(p)
    hc_s[...] = h
    hlast_ref[...] = h


def _lru_layer(x, h0, buf0, gain, w_in, conv_w, conv_b, wg0, wg1, wg2, b_r, b_i, lam, w_out, *, nb, tt, batch_major):
    if batch_major:
        total_t = x.shape[1]
        x_spec = pl.BlockSpec((nb, tt, D_MODEL), lambda i: (0, i, 0))
        x_shape = jax.ShapeDtypeStruct((nb, total_t, D_MODEL), F32)
    else:
        total_t = x.shape[0] // nb
        x_spec = pl.BlockSpec((nb * tt, D_MODEL), lambda i: (i, 0))
        x_shape = jax.ShapeDtypeStruct((nb * total_t, D_MODEL), F32)
    rows = nb * tt
    halo = (CONV_W - 1) * nb
    scratch = []
    if batch_major:
        scratch += [pltpu.VMEM((D_MODEL // LANES, rows, LANES), F32),
                    pltpu.VMEM((D_MODEL // LANES, rows, LANES), F32)]
    scratch += [pltpu.VMEM((rows + halo, W_LRU), F32),
                pltpu.VMEM((rows, W_LRU), F32),
                pltpu.VMEM((rows, W_LRU), F32),
                pltpu.VMEM((rows, W_LRU), F32),
                pltpu.VMEM((nb, W_LRU), F32)]
    return pl.pallas_call(
        functools.partial(_lru_kernel, nb, tt, batch_major),
        grid=(total_t // tt,),
        in_specs=[x_spec,
                  _const_spec((1, D_MODEL)),
                  _const_spec((D_MODEL, 2 * W_LRU)),
                  _const_spec((CONV_W, W_LRU)),
                  _const_spec((1, W_LRU)),
                  *[_const_spec((N_SUPER, hi - lo, 2 * LANES)) for lo, hi in GATE_K_RANGES],
                  _const_spec((1, W_LRU)),
                  _const_spec((1, W_LRU)),
                  _const_spec((1, W_LRU)),
                  _const_spec((W_LRU, D_MODEL)),
                  _const_spec((nb, W_LRU)),
                  _const_spec((halo, W_LRU))],
        out_specs=[x_spec,
                   pl.BlockSpec((nb, W_LRU), lambda i: (0, 0)),
                   pl.BlockSpec((halo, W_LRU), lambda i: (0, 0))],
        out_shape=[x_shape,
                   jax.ShapeDtypeStruct((nb, W_LRU), F32),
                   jax.ShapeDtypeStruct((halo, W_LRU), F32)],
        scratch_shapes=scratch,
        compiler_params=pltpu.CompilerParams(dimension_semantics=("arbitrary",),
                                             vmem_limit_bytes=VMEM_LIMIT),
        name="lru_layer_bm" if batch_major else "lru_layer_tm",
    )(x, gain, w_in, conv_w, conv_b, wg0, wg1, wg2, b_r, b_i, lam, w_out, h0, buf0)


def _lower_bound(lbl_ref):
    l0 = lbl_ref[0:1, :]
    l1 = lbl_ref[1:2, :]
    mx = jnp.maximum(l0, l1)
    e0 = jnp.exp(l0 - mx)
    e1 = jnp.exp(l1 - mx)
    return e1 / (e0 + e1)


def _hgrn_front(x, gain_ref, win_ref, lbl_ref):
    xn = _rms_scale(x, gain_ref[...]).astype(BF16)
    n = INNER_B
    q = _silu_from_half(jnp.dot(xn, win_ref[:, 0:n], preferred_element_type=F32))
    tf = jnp.tanh(jnp.dot(xn, win_ref[:, n:2 * n], preferred_element_type=F32))
    v = jnp.dot(xn, win_ref[:, 2 * n:3 * n], preferred_element_type=F32)
    gs = _silu_from_half(jnp.dot(xn, win_ref[:, 3 * n:4 * n], preferred_element_type=F32))
    lb = _lower_bound(lbl_ref)
    c1 = 0.5 * (1.0 - lb)
    ct = c1 * tf
    kk = c1 - ct
    g = (lb + c1) + ct
    lg = jnp.maximum(jnp.log(g), LOG_FLOOR)
    return q, kk, v, lg, gs


def _hgrn_back(x, o, gs, og_ref, wout_ref, fg_ref):
    parts = []
    for h in range(H_B):
        oh = o[:, h * DV_B:(h + 1) * DV_B]
        ms = jnp.mean(oh * oh, axis=-1, keepdims=True)
        parts.append(oh * lax.rsqrt(ms + EPS))
    on = jnp.concatenate(parts, axis=-1) * og_ref[...]
    y = (on * gs).astype(BF16)
    x2 = x + jnp.dot(y, wout_ref[...], preferred_element_type=F32)
    return _rms_scale(x2, fg_ref[...])


def _nt(a, b):
    return lax.dot_general(a, b, (((1,), (1,)), ((), ())), preferred_element_type=F32)


def _tn(a, b):
    return lax.dot_general(a, b, (((0,), (0,)), ((), ())), preferred_element_type=F32)


def _hgrn_prompt_kernel(tt, x_ref, gain_ref, win_ref, lbl_ref, og_ref, wout_ref, fg_ref,
                        y_ref, st_ref, q_s, k_s, v_s, cum_s, gs_s, o_s, st_s, st0_s):
    i = pl.program_id(1)
    n_chunks = tt // CHUNK
    half = CHUNK // 2
    rp = tt // HGRN_PARTS
    chunks_per_part = rp // CHUNK

    @pl.when(i == 0)
    def _():
        st_s[...] = jnp.zeros_like(st_s)

    st0_s[...] = st_s[...]

    row_c = lax.broadcasted_iota(jnp.int32, (CHUNK, CHUNK), 0)
    col_c = lax.broadcasted_iota(jnp.int32, (CHUNK, CHUNK), 1)
    tri = (row_c >= col_c).astype(BF16)
    tri3 = jnp.concatenate([tri, tri, tri], axis=-1)

    def front(p):
        r0 = p * rp
        q, kk, v, lg, gs = _hgrn_front(x_ref[r0:r0 + rp, :], gain_ref, win_ref, lbl_ref)
        q_s[r0:r0 + rp, :] = q
        k_s[r0:r0 + rp, :] = kk
        v_s[r0:r0 + rp, :] = v.astype(BF16)
        gs_s[r0:r0 + rp, :] = gs
        lg = lg * LOG2_E
        h1 = lg.astype(BF16)
        r1 = lg - h1.astype(F32)
        h2 = r1.astype(BF16)
        h3 = (r1 - h2.astype(F32)).astype(BF16)
        worst = jnp.zeros((1, INNER_B), F32)
        for c in range(chunks_per_part):
            rs = slice(c * CHUNK, (c + 1) * CHUNK)
            cum = jnp.dot(tri3, jnp.concatenate([h1[rs], h2[rs], h3[rs]], axis=0), preferred_element_type=F32)
            cum_s[r0 + c * CHUNK:r0 + (c + 1) * CHUNK, :] = cum
            mid = cum[half - 1:half, :]
            worst = jnp.maximum(worst, jnp.maximum(-mid, mid - cum[CHUNK - 1:CHUNK, :]))
        return worst

    def back(p):
        r0 = p * rp
        y_ref[r0:r0 + rp, :] = _hgrn_back(x_ref[r0:r0 + rp, :], o_s[r0:r0 + rp, :], gs_s[r0:r0 + rp, :],
                                           og_ref, wout_ref, fg_ref)

    def head_update(h, base, att, qe_b, ke_b, d):
        sl = slice(h * DK_B, (h + 1) * DK_B)
        vh = v_s[pl.ds(base, CHUNK), sl]
        st = st_s[h]
        o = _nt(qe_b[:, sl], st.astype(BF16)) + jnp.dot(att.astype(BF16), vh, preferred_element_type=F32)
        st_s[h] = d[:, sl] * st + _tn(vh, ke_b[:, sl])
        o_s[pl.ds(base, CHUNK), sl] = o

    def chunk_common(base):
        cu = cum_s[pl.ds(base, CHUNK), :]
        tail = cum_s[pl.ds(base + CHUNK - 8, 8), :]
        clast = tail[7:8, :]
        qf = q_s[pl.ds(base, CHUNK), :]
        kf = k_s[pl.ds(base, CHUNK), :]
        qe = qf * jnp.exp2(cu)
        ke_b = (kf * jnp.exp2(clast - cu)).astype(BF16)
        return cu, qf, kf, qe, ke_b, jnp.exp2(clast)

    def fast_chunk(base):
        cu_l = cum_s[pl.ds(base, half), :]
        cu_r = cum_s[pl.ds(base + half, half), :]
        q_l, q_r = q_s[pl.ds(base, half), :], q_s[pl.ds(base + half, half), :]
        k_l, k_r = k_s[pl.ds(base, half), :], k_s[pl.ds(base + half, half), :]
        cmid = cu_l[half - 1:half, :]
        clast = cu_r[half - 1:half, :]
        bf = lambda z: z.astype(BF16)
        qe_l, qe_r = bf(q_l * jnp.exp2(cu_l)), bf(q_r * jnp.exp2(cu_r))
        ke_l, ke_r = bf(k_l * jnp.exp2(clast - cu_l)), bf(k_r * jnp.exp2(clast - cu_r))
        za_l = bf(k_l * jnp.exp2(cmid - cu_l))
        za_r = bf(q_r * jnp.exp2(cu_r - cmid))
        kl_l = bf(k_l * jnp.exp2(-cu_l))
        kl_r = bf(k_r * jnp.exp2(cmid - cu_r))
        qe_b = jnp.concatenate([qe_l, qe_r], axis=0)
        ke_b = jnp.concatenate([ke_l, ke_r], axis=0)
        za_b = jnp.concatenate([za_l, za_r], axis=0)
        kl_b = jnp.concatenate([kl_l, kl_r], axis=0)
        ql_b = jnp.concatenate([qe_l, za_r], axis=0)
        d = jnp.exp2(clast)
        cross = (row_c >= half) & (col_c < half)
        local = (row_c >= col_c) & ((row_c >= half) == (col_c >= half))
        for h in range(H_B):
            sl = slice(h * DK_B, (h + 1) * DK_B)
            pa = _nt(za_b[:, sl], za_b[:, sl])
            pb = _nt(ql_b[:, sl], kl_b[:, sl])
            att = jnp.where(cross, pa, jnp.where(local, pb, 0.0))
            head_update(h, base, att, qe_b, ke_b, d)

    def slow_chunk(c, carry):
        base = pl.multiple_of(c * CHUNK, CHUNK)
        cu_all, qf, kf, qe, ke_b, d = chunk_common(base)
        qe_b = qe.astype(BF16)
        xor_c = row_c ^ col_c
        row_l = lax.broadcasted_iota(jnp.int32, (CHUNK, LANES), 0)
        sub8 = lax.broadcasted_iota(jnp.int32, (8, LANES), 0)
        for h in range(H_B):
            sl = slice(h * DK_B, (h + 1) * DK_B)
            cu = cu_all[:, sl]
            qh = qf[:, sl]
            kh = kf[:, sl]

            def ref_row(r):
                grp = cu[(r // 8) * 8:(r // 8) * 8 + 8, :]
                return grp[r % 8:r % 8 + 1, :]

            att = jnp.zeros((CHUNK, CHUNK), F32)
            for lm in range(LOG_CHUNK):
                m = 1 << lm
                if lm >= 2:
                    cm = jnp.concatenate(
                        [jnp.broadcast_to(ref_row(blk * 2 * m + m - 1), (2 * m, LANES))
                         for blk in range(CHUNK // (2 * m))], axis=0)
                elif lm == 1:
                    cm = jnp.concatenate(
                        [jnp.where(sub8 < 4,
                                   jnp.broadcast_to(ref_row(blk * 8 + 1), (8, LANES)),
                                   jnp.broadcast_to(ref_row(blk * 8 + 5), (8, LANES)))
                         for blk in range(CHUNK // 8)], axis=0)
                else:
                    cm = jnp.where((row_l & 1) == 1, pltpu.roll(cu, 1, axis=0), cu)
                right = ((row_l >> lm) & 1) == 1
                z = (jnp.where(right, qh, kh) * jnp.exp2(-jnp.abs(cu - cm))).astype(BF16)
                pair = ((xor_c >> lm) == 1) & (((row_c >> lm) & 1) == 1)
                att = jnp.where(pair, _nt(z, z), att)
            att = jnp.where(row_c == col_c, _nt(qh.astype(BF16), kh.astype(BF16)), att)
            head_update(h, base, att, qe_b, ke_b, d)
        return carry

    worst = front(0)
    for p in range(1, HGRN_PARTS):
        worst = jnp.maximum(worst, front(p))
    for c in range(n_chunks):
        fast_chunk(c * CHUNK)
    for p in range(HGRN_PARTS):
        back(p)

    @pl.when(jnp.max(worst) > MAX_LOCAL_EXPONENT * LOG2_E)
    def _():
        st_s[...] = st0_s[...]
        lax.fori_loop(0, n_chunks, slow_chunk, 0)
        for p in range(HGRN_PARTS):
            back(p)

    @pl.when(i == pl.num_programs(1) - 1)
    def _():
        for h in range(H_B):
            st_ref[h] = st_s[h].T


def _hgrn_prompt(x1, gain, w_in, lbl, o_gain, w_out, f_gain, *, tt):
    nb, total_t, _ = x1.shape
    return pl.pallas_call(
        functools.partial(_hgrn_prompt_kernel, tt),
        grid=(nb, total_t // tt),
        in_specs=[pl.BlockSpec((None, tt, D_MODEL), lambda b, i: (b, i, 0)),
                  _const_spec((1, D_MODEL)),
                  _const_spec((D_MODEL, 4 * INNER_B)),
                  _const_spec((2, INNER_B)),
                  _const_spec((1, INNER_B)),
                  _const_spec((INNER_B, D_MODEL)),
                  _const_spec((1, D_MODEL))],
        out_specs=[pl.BlockSpec((None, tt, D_MODEL), lambda b, i: (b, i, 0)),
                   pl.BlockSpec((None, H_B, DK_B, DV_B), lambda b, i: (b, 0, 0, 0))],
        out_shape=[jax.ShapeDtypeStruct((nb, total_t, D_MODEL), F32),
                   jax.ShapeDtypeStruct((nb, H_B, DK_B, DV_B), F32)],
        scratch_shapes=[pltpu.VMEM((tt, INNER_B), F32),
                        pltpu.VMEM((tt, INNER_B), F32),
                        pltpu.VMEM((tt, INNER_B), BF16),
                        pltpu.VMEM((tt, INNER_B), F32),
                        pltpu.VMEM((tt, INNER_B), F32),
                        pltpu.VMEM((tt, INNER_B), F32),
                        pltpu.VMEM((H_B, DV_B, DK_B), F32),
                        pltpu.VMEM((H_B, DV_B, DK_B), F32)],
        compiler_params=pltpu.CompilerParams(dimension_semantics=("arbitrary", "arbitrary"),
                                             vmem_limit_bytes=VMEM_LIMIT),
        name="hgrn_prompt",
    )(x1, gain, w_in, lbl, o_gain, w_out, f_gain)


def _hgrn_decode_kernel(nb, tt, bstep, x_ref, gain_ref, win_ref, lbl_ref, og_ref, wout_ref, fg_ref, s0_ref,
                        y_ref, snew_ref, qe_s, ke_s, dd_s, v_s, oi_s, gs_s, x_s, o_s):
    j = pl.program_id(0)
    rows = nb * tt

    def to_batch_major(dst, val, t):
        for h in range(H_B):
            dst[h, pl.ds(t, nb, stride=tt), :] = val[:, h * LANES:(h + 1) * LANES]

    @pl.when(j == 0)
    def _():
        x = x_ref[...]
        q, kk, v, lg, gs = _hgrn_front(x, gain_ref, win_ref, lbl_ref)
        ones_blk = jnp.ones((DK_B, DK_B), BF16)

        def slab(arr, t):
            return arr[t * nb:(t + 1) * nb, :]

        cum = [slab(lg, 0)]
        for t in range(1, tt):
            cum.append(cum[-1] + slab(lg, t))
        dd_s[...] = jnp.zeros_like(dd_s)
        d = jnp.exp(cum[tt - 1])
        d_hi = d.astype(BF16).astype(F32)
        to_batch_major(dd_s, d_hi, 0)
        to_batch_major(dd_s, d - d_hi, 1)
        for t in range(tt):
            qt = slab(q, t)
            acc = jnp.zeros((nb, INNER_B), F32)
            for s in range(t + 1):
                prod = qt * slab(kk, s)
                if s < t:
                    prod = prod * jnp.exp(cum[t] - cum[s])
                pb = prod.astype(BF16)
                att = jnp.concatenate(
                    [jnp.dot(pb[:, h * DK_B:(h + 1) * DK_B], ones_blk, preferred_element_type=F32)
                     for h in range(H_B)], axis=-1)
                acc = acc + att * slab(v, s)
            to_batch_major(oi_s, acc, t)
            to_batch_major(qe_s, qt * jnp.exp(cum[t]), t)
            to_batch_major(ke_s, slab(kk, t) * jnp.exp(cum[tt - 1] - cum[t]), t)
            to_batch_major(v_s, slab(v, t), t)
            to_batch_major(gs_s, slab(gs, t), t)
            to_batch_major(x_s, slab(x, t), t)

    row1 = lax.broadcasted_iota(jnp.int32, (8, LANES), 0)
    lane2 = lax.broadcasted_iota(jnp.int32, (8, 2 * DV_B), 1)
    row2 = lax.broadcasted_iota(jnp.int32, (8, 2 * DV_B), 0)
    own = (row1 < tt, row1 >= tt)
    ones_rows = tuple(
        jnp.where((lane2 >= DV_B) & (row2 >= lo) & (row2 < lo + 2), 1.0, 0.0).astype(BF16) for lo in (tt, 0))

    def pair_body(pi, carry):
        off = pl.multiple_of((j * (bstep // 2) + pi) * 8, 8)
        for h in range(H_B):
            qe = qe_s[h, pl.ds(off, 8), :].astype(BF16)
            ke = ke_s[h, pl.ds(off, 8), :]
            dd = pltpu.roll(dd_s[h, pl.ds(off, 8), :], tt, axis=0)
            vv = v_s[h, pl.ds(off, 8), :]
            inter = []
            for e in range(2):
                s0 = s0_ref[2 * pi + e, h]
                inter.append(jnp.dot(qe, s0.astype(BF16), preferred_element_type=F32))
                lhs = jnp.where(own[e], ke, dd).astype(BF16)
                ve = jnp.where(own[e], vv, 0.0).astype(BF16)
                rhs = jnp.concatenate([ve, jnp.zeros((8, DV_B), BF16)], axis=-1) + ones_rows[e]
                upd = lax.dot_general(lhs, rhs, (((0,), (0,)), ((), ())),
                                      preferred_element_type=F32)
                snew_ref[2 * pi + e, h] = upd[:, DV_B:] * s0 + upd[:, :DV_B]
            o_s[pl.ds(off, 8), h * DV_B:(h + 1) * DV_B] = (
                jnp.where(own[0], inter[0], inter[1]) + oi_s[h, pl.ds(off, 8), :])
        return carry

    lax.fori_loop(0, bstep // 2, pair_body, 0, unroll=True)

    @pl.when(j == pl.num_programs(0) - 1)
    def _():
        xb = jnp.concatenate([x_s[h] for h in range(H_B)], axis=-1)
        gsb = jnp.concatenate([gs_s[h] for h in range(H_B)], axis=-1)
        y_ref[...] = _hgrn_back(xb, o_s[...], gsb, og_ref, wout_ref, fg_ref)


def _hgrn_decode(x1, s0, gain, w_in, lbl, o_gain, w_out, f_gain, *, nb, tt, bstep):
    assert 2 * tt == 8 and bstep % 2 == 0
    rows = nb * tt
    slab = pltpu.VMEM((H_B, rows, LANES), F32)
    return pl.pallas_call(
        functools.partial(_hgrn_decode_kernel, nb, tt, bstep),
        grid=(nb // bstep,),
        in_specs=[_const_spec((rows, D_MODEL)),
                  _const_spec((1, D_MODEL)),
                  _const_spec((D_MODEL, 4 * INNER_B)),
                  _const_spec((2, INNER_B)),
                  _const_spec((1, INNER_B)),
                  _const_spec((INNER_B, D_MODEL)),
                  _const_spec((1, D_MODEL)),
                  pl.BlockSpec((bstep, H_B, DK_B, DV_B), lambda j: (j, 0, 0, 0))],
        out_specs=[pl.BlockSpec((rows, D_MODEL), lambda j: (0, 0)),
                   pl.BlockSpec((bstep, H_B, DK_B, DV_B), lambda j: (j, 0, 0, 0))],
        out_shape=[jax.ShapeDtypeStruct((rows, D_MODEL), F32),
                   jax.ShapeDtypeStruct((nb, H_B, DK_B, DV_B), F32)],
        scratch_shapes=[slab, slab, slab, slab, slab, slab, slab,
                        pltpu.VMEM((rows, INNER_B), F32)],
        compiler_params=pltpu.CompilerParams(dimension_semantics=("arbitrary",),
                                             vmem_limit_bytes=VMEM_LIMIT),
        name="hgrn_decode",
    )(x1, gain, w_in, lbl, o_gain, w_out, f_gain, s0)


def _gate_weights(w_r, w_i):
    per = GATE_SUPER // BLK_W

    def block_diag(w):
        w4 = w.reshape(N_SUPER, per, BLK_W, BLK_W)
        eye = jnp.eye(per, dtype=w.dtype)
        return jnp.einsum('saij,ac->saicj', w4, eye).reshape(N_SUPER, GATE_SUPER, GATE_SUPER)

    dr, di = block_diag(w_r), block_diag(w_i)
    out = []
    for j, (lo, hi) in enumerate(GATE_K_RANGES):
        cols = slice(j * LANES, (j + 1) * LANES)
        out.append(jnp.concatenate([dr[:, lo:hi, cols], di[:, lo:hi, cols]], axis=-1).astype(BF16))
    return out


def kernel(x_prompt, x_sample, state_lru_h, state_lru_conv, state_hgrn, norm_gain, a_w_in, a_conv_w, a_conv_b,
           a_w_r, a_b_r, a_w_i, a_b_i, a_lambda, a_w_out, b_w_in, b_lb_logits, b_o_gain, b_w_out, final_gain):
    assert norm_gain.shape[0] == 2 and a_w_in.shape[0] == 1 and b_w_in.shape[0] == 1
    pb, pt, _ = x_prompt.shape
    sb, st, _ = x_sample.shape
    halo_t = CONV_W - 1

    row = lambda p: p.reshape(1, -1)
    lru_col_scale = jnp.concatenate([jnp.ones((W_LRU,), F32), jnp.full((W_LRU,), 0.5, F32)])
    hgrn_col_scale = jnp.concatenate([jnp.full((2 * INNER_B,), 0.5, F32), jnp.ones((INNER_B,), F32),
                                      jnp.full((INNER_B,), 0.5, F32)])
    lru_params = (row(norm_gain[0]), (a_w_in[0] * lru_col_scale).astype(BF16), 0.5 * a_conv_w[0],
                  row(0.5 * a_conv_b[0]), *_gate_weights(a_w_r[0], a_w_i[0]), row(0.5 * a_b_r[0]),
                  row(0.5 * a_b_i[0]), row(a_lambda[0]), a_w_out[0].astype(BF16))
    hgrn_params = (row(norm_gain[1]), (b_w_in[0] * hgrn_col_scale).astype(BF16), b_lb_logits, row(b_o_gain[0]),
                   b_w_out[0].astype(BF16), row(final_gain))

    x1p, hp, bufp = _lru_layer(x_prompt, jnp.zeros((pb, W_LRU), F32), jnp.zeros((halo_t * pb, W_LRU), F32),
                               *lru_params, nb=pb, tt=64, batch_major=True)
    y_prompt, sp = _hgrn_prompt(x1p, *hgrn_params, tt=512)
    bufp = bufp.reshape(halo_t, pb, W_LRU).transpose(1, 0, 2)

    xs = x_sample.transpose(1, 0, 2).reshape(st * sb, D_MODEL)
    bufs0 = state_lru_conv[0].transpose(1, 0, 2).reshape(halo_t * sb, W_LRU)
    x1s, hs, bufs = _lru_layer(xs, state_lru_h[0], bufs0, *lru_params, nb=sb, tt=st, batch_major=False)
    ys, ss = _hgrn_decode(x1s, state_hgrn[0], *hgrn_params, nb=sb, tt=st, bstep=4)
    bufs = bufs.reshape(halo_t, sb, W_LRU).transpose(1, 0, 2)

    return (y_prompt, ys.reshape(sb, st, D_MODEL), hp[None], bufp[None], sp[None],
            hs[None], bufs[None], ss[None])
```

```python
import functools

import jax
import jax.numpy as jnp
from jax import lax
from jax.experimental import pallas as pl
from jax.experimental.pallas import tpu as pltpu

F32 = jnp.float32
BF16 = jnp.bfloat16

D_MODEL = 1024
W_LRU = 1536
N_BLK = 16
BLK_W = W_LRU // N_BLK
CONV_W = 4
LRU_C = 8.0
H_B = 8
DK_B = 128
DV_B = 128
INNER_B = H_B * DK_B
EPS = 1e-6

LANES = 128
GATE_SUPER = 384
N_SUPER = W_LRU // GATE_SUPER
LRU_PARTS = 2
HGRN_PARTS = 2
CHUNK = 64
LOG_CHUNK = 6
VMEM_LIMIT = 60 * 1024 * 1024
MAX_LOCAL_EXPONENT = 75.0
LOG2_E = 1.4426950408889634
LOG_FLOOR = -1e4


def _silu_from_half(hx):
    return hx + hx * jnp.tanh(hx)


def _rms_scale(x, gain):
    ms = jnp.mean(x * x, axis=-1, keepdims=True)
    return x * lax.rsqrt(ms + EPS) * gain


def _const_spec(shape):
    zeros = (0,) * len(shape)
    return pl.BlockSpec(shape, lambda *_: zeros, pipeline_mode=pl.Buffered(1))


def _gate_k_ranges():
    ranges = []
    for j in range(GATE_SUPER // LANES):
        first_blk = (j * LANES) // BLK_W
        last_blk = ((j + 1) * LANES - 1) // BLK_W
        lo = (first_blk * BLK_W) // LANES * LANES
        hi = -(-((last_blk + 1) * BLK_W) // LANES) * LANES
        ranges.append((lo, hi))
    return ranges


GATE_K_RANGES = _gate_k_ranges()


def _lru_kernel(nb, tt, batch_major, x_ref, gain_ref, win_ref, cw_ref, cb_ref, wg0_ref, wg1_ref, wg2_ref,
                br_ref, bi_ref, lam_ref, wout_ref, h0_ref, buf0_ref, x1_ref, hlast_ref, bufout_ref, *scratch):
    wg_refs = (wg0_ref, wg1_ref, wg2_ref)
    if batch_major:
        xs_s, os_s, xbuf_s, a_s, b_s, g_s, hc_s = scratch
    else:
        xbuf_s, a_s, b_s, g_s, hc_s = scratch
    rows = nb * tt
    halo = (CONV_W - 1) * nb
    n_lane_tiles = D_MODEL // LANES
    tp = tt // LRU_PARTS
    rp = nb * tp

    @pl.when(pl.program_id(0) == 0)
    def _():
        hc_s[...] = h0_ref[...]
        xbuf_s[0:halo, :] = buf0_ref[...]

    if batch_major:
        for b in range(nb):
            for j in range(n_lane_tiles):
                xs_s[j, pl.ds(b, tt, stride=nb), :] = x_ref[b, :, j * LANES:(j + 1) * LANES]

    def load_x(p):
        if batch_major:
            return jnp.concatenate([xs_s[j, p * rp:(p + 1) * rp, :] for j in range(n_lane_tiles)], axis=-1)
        return x_ref[p * rp:(p + 1) * rp, :]

    nlam = -lam_ref[...]
    softplus = jnp.maximum(nlam, 0.0) + jnp.log1p(jnp.exp(-jnp.abs(nlam)))
    half_rate = (-0.5 * LRU_C) * softplus

    def pre(p):
        r0 = p * rp
        xn = _rms_scale(load_x(p), gain_ref[...]).astype(BF16)
        u = jnp.dot(xn, win_ref[...], preferred_element_type=F32)
        g_s[r0:r0 + rp, :] = _silu_from_half(u[:, W_LRU:])
        xbuf_s[halo + r0:halo + r0 + rp, :] = u[:, :W_LRU]
        xh = cb_ref[...] + cw_ref[0:1, :] * xbuf_s[r0:r0 + rp, :]
        for k in range(1, CONV_W):
            xh = xh + cw_ref[k:k + 1, :] * xbuf_s[r0 + k * nb:r0 + k * nb + rp, :]
        xhb = xh.astype(BF16)
        r_parts, i_parts = [], []
        for s in range(N_SUPER):
            for (lo, hi), wg_ref in zip(GATE_K_RANGES, wg_refs):
                gsi = jnp.dot(xhb[:, s * GATE_SUPER + lo:s * GATE_SUPER + hi], wg_ref[s],
                              preferred_element_type=F32)
                r_parts.append(gsi[:, :LANES])
                i_parts.append(gsi[:, LANES:])
        tr = jnp.tanh(jnp.concatenate(r_parts, axis=-1) + br_ref[...])
        ti = jnp.tanh(jnp.concatenate(i_parts, axis=-1) + bi_ref[...])
        log_a = half_rate + half_rate * tr
        a = jnp.exp(log_a)
        one_m_a2 = -jnp.tanh(log_a) * (a * a + 1.0)
        a_s[r0:r0 + rp, :] = a
        root = jnp.where(one_m_a2 > 0.0, one_m_a2 * lax.rsqrt(one_m_a2), 0.0)
        b_s[r0:r0 + rp, :] = root * (xh * (1.0 + ti))

    def scan(p, h):
        for t in range(p * tp, (p + 1) * tp):
            h = a_s[t * nb:(t + 1) * nb, :] * h + b_s[t * nb:(t + 1) * nb, :]
            b_s[t * nb:(t + 1) * nb, :] = h
        return h

    def post(p):
        r0 = p * rp
        y = (b_s[r0:r0 + rp, :] * g_s[r0:r0 + rp, :]).astype(BF16)
        out = load_x(p) + jnp.dot(y, wout_ref[...], preferred_element_type=F32)
        if batch_major:
            for j in range(n_lane_tiles):
                os_s[j, r0:r0 + rp, :] = out[:, j * LANES:(j + 1) * LANES]
            for b in range(nb):
                x1_ref[b, p * tp:(p + 1) * tp, :] = jnp.concatenate(
                    [os_s[j, pl.ds(r0 + b, tp, stride=nb), :] for j in range(n_lane_tiles)], axis=-1)
        else:
            x1_ref[r0:r0 + rp, :] = out

    for p in range(LRU_PARTS):
        pre(p)
    new_halo = xbuf_s[rows:rows + halo, :]
    bufout_ref[...] = new_halo
    xbuf_s[0:halo, :] = new_halo
    h = hc_s[...]
    for p in range(LRU_PARTS):
        h = scan(p, h)
        post(p)
    hc_s[...] = h
    hlast_ref[...] = h


def _lru_layer(x, h0, buf0, gain, w_in, conv_w, conv_b, wg0, wg1, wg2, b_r, b_i, lam, w_out, *, nb, tt, batch_major):
    if batch_major:
        total_t = x.shape[1]
        x_spec = pl.BlockSpec((nb, tt, D_MODEL), lambda i: (0, i, 0))
        x_shape = jax.ShapeDtypeStruct((nb, total_t, D_MODEL), F32)
    else:
        total_t = x.shape[0] // nb
        x_spec = pl.BlockSpec((nb * tt, D_MODEL), lambda i: (i, 0))
        x_shape = jax.ShapeDtypeStruct((nb * total_t, D_MODEL), F32)
    rows = nb * tt
    halo = (CONV_W - 1) * nb
    scratch = []
    if batch_major:
        scratch += [pltpu.VMEM((D_MODEL // LANES, rows, LANES), F32),
                    pltpu.VMEM((D_MODEL // LANES, rows, LANES), F32)]
    scratch += [pltpu.VMEM((rows + halo, W_LRU), F32),
                pltpu.VMEM((rows, W_LRU), F32),
                pltpu.VMEM((rows, W_LRU), F32),
                pltpu.VMEM((rows, W_LRU), F32),
                pltpu.VMEM((nb, W_LRU), F32)]
    return pl.pallas_call(
        functools.partial(_lru_kernel, nb, tt, batch_major),
        grid=(total_t // tt,),
        in_specs=[x_spec,
                  _const_spec((1, D_MODEL)),
                  _const_spec((D_MODEL, 2 * W_LRU)),
                  _const_spec((CONV_W, W_LRU)),
                  _const_spec((1, W_LRU)),
                  *[_const_spec((N_SUPER, hi - lo, 2 * LANES)) for lo, hi in GATE_K_RANGES],
                  _const_spec((1, W_LRU)),
                  _const_spec((1, W_LRU)),
                  _const_spec((1, W_LRU)),
                  _const_spec((W_LRU, D_MODEL)),
                  _const_spec((nb, W_LRU)),
                  _const_spec((halo, W_LRU))],
        out_specs=[x_spec,
                   pl.BlockSpec((nb, W_LRU), lambda i: (0, 0)),
                   pl.BlockSpec((halo, W_LRU), lambda i: (0, 0))],
        out_shape=[x_shape,
                   jax.ShapeDtypeStruct((nb, W_LRU), F32),
                   jax.ShapeDtypeStruct((halo, W_LRU), F32)],
        scratch_shapes=scratch,
        compiler_params=pltpu.CompilerParams(dimension_semantics=("arbitrary",),
                                             vmem_limit_bytes=VMEM_LIMIT),
        name="lru_layer_bm" if batch_major else "lru_layer_tm",
    )(x, gain, w_in, conv_w, conv_b, wg0, wg1, wg2, b_r, b_i, lam, w_out, h0, buf0)


def _lower_bound(lbl_ref):
    l0 = lbl_ref[0:1, :]
    l1 = lbl_ref[1:2, :]
    mx = jnp.maximum(l0, l1)
    e0 = jnp.exp(l0 - mx)
    e1 = jnp.exp(l1 - mx)
    return e1 / (e0 + e1)


def _hgrn_front(x, gain_ref, win_ref, lbl_ref):
    xn = _rms_scale(x, gain_ref[...]).astype(BF16)
    n = INNER_B
    q = _silu_from_half(jnp.dot(xn, win_ref[:, 0:n], preferred_element_type=F32))
    tf = jnp.tanh(jnp.dot(xn, win_ref[:, n:2 * n], preferred_element_type=F32))
    v = jnp.dot(xn, win_ref[:, 2 * n:3 * n], preferred_element_type=F32)
    gs = _silu_from_half(jnp.dot(xn, win_ref[:, 3 * n:4 * n], preferred_element_type=F32))
    lb = _lower_bound(lbl_ref)
    c1 = 0.5 * (1.0 - lb)
    ct = c1 * tf
    kk = c1 - ct
    g = (lb + c1) + ct
    lg = jnp.maximum(jnp.log(g), LOG_FLOOR)
    return q, kk, v, lg, gs


def _hgrn_back(x, o, gs, og_ref, wout_ref, fg_ref):
    parts = []
    for h in range(H_B):
        oh = o[:, h * DV_B:(h + 1) * DV_B]
        ms = jnp.mean(oh * oh, axis=-1, keepdims=True)
        parts.append(oh * lax.rsqrt(ms + EPS))
    on = jnp.concatenate(parts, axis=-1) * og_ref[...]
    y = (on * gs).astype(BF16)
    x2 = x + jnp.dot(y, wout_ref[...], preferred_element_type=F32)
    return _rms_scale(x2, fg_ref[...])


def _nt(a, b):
    return lax.dot_general(a, b, (((1,), (1,)), ((), ())), preferred_element_type=F32)


def _tn(a, b):
    return lax.dot_general(a, b, (((0,), (0,)), ((), ())), preferred_element_type=F32)


def _hgrn_prompt_kernel(tt, x_ref, gain_ref, win_ref, lbl_ref, og_ref, wout_ref, fg_ref,
                        y_ref, st_ref, q_s, k_s, v_s, cum_s, gs_s, o_s, st_s, st0_s):
    i = pl.program_id(1)
    n_chunks = tt // CHUNK
    half = CHUNK // 2
    rp = tt // HGRN_PARTS
    chunks_per_part = rp // CHUNK

    @pl.when(i == 0)
    def _():
        st_s[...] = jnp.zeros_like(st_s)

    st0_s[...] = st_s[...]

    row_c = lax.broadcasted_iota(jnp.int32, (CHUNK, CHUNK), 0)
    col_c = lax.broadcasted_iota(jnp.int32, (CHUNK, CHUNK), 1)
    tri = (row_c >= col_c).astype(BF16)
    tri3 = jnp.concatenate([tri, tri, tri], axis=-1)

    def front(p):
        r0 = p * rp
        q, kk, v, lg, gs = _hgrn_front(x_ref[r0:r0 + rp, :], gain_ref, win_ref, lbl_ref)
        q_s[r0:r0 + rp, :] = q
        k_s[r0:r0 + rp, :] = kk
        v_s[r0:r0 + rp, :] = v.astype(BF16)
        gs_s[r0:r0 + rp, :] = gs
        lg = lg * LOG2_E
        h1 = lg.astype(BF16)
        r1 = lg - h1.astype(F32)
        h2 = r1.astype(BF16)
        h3 = (r1 - h2.astype(F32)).astype(BF16)
        worst = jnp.zeros((1, INNER_B), F32)
        for c in range(chunks_per_part):
            rs = slice(c * CHUNK, (c + 1) * CHUNK)
            cum = jnp.dot(tri3, jnp.concatenate([h1[rs], h2[rs], h3[rs]], axis=0), preferred_element_type=F32)
            cum_s[r0 + c * CHUNK:r0 + (c + 1) * CHUNK, :] = cum
            mid = cum[half - 1:half, :]
            worst = jnp.maximum(worst, jnp.maximum(-mid, mid - cum[CHUNK - 1:CHUNK, :]))
        return worst

    def back(p):
        r0 = p * rp
        y_ref[r0:r0 + rp, :] = _hgrn_back(x_ref[r0:r0 + rp, :], o_s[r0:r0 + rp, :], gs_s[r0:r0 + rp, :],
                                           og_ref, wout_ref, fg_ref)

    def head_update(h, base, att, qe_b, ke_b, d):
        sl = slice(h * DK_B, (h + 1) * DK_B)
        vh = v_s[pl.ds(base, CHUNK), sl]
        st = st_s[h]
        o = _nt(qe_b[:, sl], st.astype(BF16)) + jnp.dot(att.astype(BF16), vh, preferred_element_type=F32)
        st_s[h] = d[:, sl] * st + _tn(vh, ke_b[:, sl])
        o_s[pl.ds(base, CHUNK), sl] = o

    def chunk_common(base):
        cu = cum_s[pl.ds(base, CHUNK), :]
        tail = cum_s[pl.ds(base + CHUNK - 8, 8), :]
        clast = tail[7:8, :]
        qf = q_s[pl.ds(base, CHUNK), :]
        kf = k_s[pl.ds(base, CHUNK), :]
        qe = qf * jnp.exp2(cu)
        ke_b = (kf * jnp.exp2(clast - cu)).astype(BF16)
        return cu, qf, kf, qe, ke_b, jnp.exp2(clast)

    def fast_chunk(base):
        cu_l = cum_s[pl.ds(base, half), :]
        cu_r = cum_s[pl.ds(base + half, half), :]
        q_l, q_r = q_s[pl.ds(base, half), :], q_s[pl.ds(base + half, half), :]
        k_l, k_r = k_s[pl.ds(base, half), :], k_s[pl.ds(base + half, half), :]
        cmid = cu_l[half - 1:half, :]
        clast = cu_r[half - 1:half, :]
        bf = lambda z: z.astype(BF16)
        qe_l, qe_r = bf(q_l * jnp.exp2(cu_l)), bf(q_r * jnp.exp2(cu_r))
        ke_l, ke_r = bf(k_l * jnp.exp2(clast - cu_l)), bf(k_r * jnp.exp2(clast - cu_r))
        za_l = bf(k_l * jnp.exp2(cmid - cu_l))
        za_r = bf(q_r * jnp.exp2(cu_r - cmid))
        kl_l = bf(k_l * jnp.exp2(-cu_l))
        kl_r = bf(k_r * jnp.exp2(cmid - cu_r))
        qe_b = jnp.concatenate([qe_l, qe_r], axis=0)
        ke_b = jnp.concatenate([ke_l, ke_r], axis=0)
        za_b = jnp.concatenate([za_l, za_r], axis=0)
        kl_b = jnp.concatenate([kl_l, kl_r], axis=0)
        ql_b = jnp.concatenate([qe_l, za_r], axis=0)
        d = jnp.exp2(clast)
        cross = (row_c >= half) & (col_c < half)
        local = (row_c >= col_c) & ((row_c >= half) == (col_c >= half))
        for h in range(H_B):
            sl = slice(h * DK_B, (h + 1) * DK_B)
            pa = _nt(za_b[:, sl], za_b[:, sl])
            pb = _nt(ql_b[:, sl], kl_b[:, sl])
            att = jnp.where(cross, pa, jnp.where(local, pb, 0.0))
            head_update(h, base, att, qe_b, ke_b, d)

    def slow_chunk(c, carry):
        base = pl.multiple_of(c * CHUNK, CHUNK)
        cu_all, qf, kf, qe, ke_b, d = chunk_common(base)
        qe_b = qe.astype(BF16)
        xor_c = row_c ^ col_c
        row_l = lax.broadcasted_iota(jnp.int32, (CHUNK, LANES), 0)
        sub8 = lax.broadcasted_iota(jnp.int32, (8, LANES), 0)
        for h in range(H_B):
            sl = slice(h * DK_B, (h + 1) * DK_B)
            cu = cu_all[:, sl]
            qh = qf[:, sl]
            kh = kf[:, sl]

            def ref_row(r):
                grp = cu[(r // 8) * 8:(r // 8) * 8 + 8, :]
                return grp[r % 8:r % 8 + 1, :]

            att = jnp.zeros((CHUNK, CHUNK), F32)
            for lm in range(LOG_CHUNK):
                m = 1 << lm
                if lm >= 2:
                    cm = jnp.concatenate(
                        [jnp.broadcast_to(ref_row(blk * 2 * m + m - 1), (2 * m, LANES))
                         for blk in range(CHUNK // (2 * m))], axis=0)
                elif lm == 1:
                    cm = jnp.concatenate(
                        [jnp.where(sub8 < 4,
                                   jnp.broadcast_to(ref_row(blk * 8 + 1), (8, LANES)),
                                   jnp.broadcast_to(ref_row(blk * 8 + 5), (8, LANES)))
                         for blk in range(CHUNK // 8)], axis=0)
                else:
                    cm = jnp.where((row_l & 1) == 1, pltpu.roll(cu, 1, axis=0), cu)
                right = ((row_l >> lm) & 1) == 1
                z = (jnp.where(right, qh, kh) * jnp.exp2(-jnp.abs(cu - cm))).astype(BF16)
                pair = ((xor_c >> lm) == 1) & (((row_c >> lm) & 1) == 1)
                att = jnp.where(pair, _nt(z, z), att)
            att = jnp.where(row_c == col_c, _nt(qh.astype(BF16), kh.astype(BF16)), att)
            head_update(h, base, att, qe_b, ke_b, d)
        return carry

    worst = front(0)
    for p in range(1, HGRN_PARTS):
        worst = jnp.maximum(worst, front(p))
    for c in range(n_chunks):
        fast_chunk(c * CHUNK)
    for p in range(HGRN_PARTS):
        back(p)

    @pl.when(jnp.max(worst) > MAX_LOCAL_EXPONENT * LOG2_E)
    def _():
        st_s[...] = st0_s[...]
        lax.fori_loop(0, n_chunks, slow_chunk, 0)
        for p in range(HGRN_PARTS):
            back(p)

    @pl.when(i == pl.num_programs(1) - 1)
    def _():
        for h in range(H_B):
            st_ref[h] = st_s[h].T


def _hgrn_prompt(x1, gain, w_in, lbl, o_gain, w_out, f_gain, *, tt):
    nb, total_t, _ = x1.shape
    return pl.pallas_call(
        functools.partial(_hgrn_prompt_kernel, tt),
        grid=(nb, total_t // tt),
        in_specs=[pl.BlockSpec((None, tt, D_MODEL), lambda b, i: (b, i, 0)),
                  _const_spec((1, D_MODEL)),
                  _const_spec((D_MODEL, 4 * INNER_B)),
                  _const_spec((2, INNER_B)),
                  _const_spec((1, INNER_B)),
                  _const_spec((INNER_B, D_MODEL)),
                  _const_spec((1, D_MODEL))],
        out_specs=[pl.BlockSpec((None, tt, D_MODEL), lambda b, i: (b, i, 0)),
                   pl.BlockSpec((None, H_B, DK_B, DV_B), lambda b, i: (b, 0, 0, 0))],
        out_shape=[jax.ShapeDtypeStruct((nb, total_t, D_MODEL), F32),
                   jax.ShapeDtypeStruct((nb, H_B, DK_B, DV_B), F32)],
        scratch_shapes=[pltpu.VMEM((tt, INNER_B), F32),
                        pltpu.VMEM((tt, INNER_B), F32),
                        pltpu.VMEM((tt, INNER_B), BF16),
                        pltpu.VMEM((tt, INNER_B), F32),
                        pltpu.VMEM((tt, INNER_B), F32),
                        pltpu.VMEM((tt, INNER_B), F32),
                        pltpu.VMEM((H_B, DV_B, DK_B), F32),
                        pltpu.VMEM((H_B, DV_B, DK_B), F32)],
        compiler_params=pltpu.CompilerParams(dimension_semantics=("arbitrary", "arbitrary"),
                                             vmem_limit_bytes=VMEM_LIMIT),
        name="hgrn_prompt",
    )(x1, gain, w_in, lbl, o_gain, w_out, f_gain)


def _hgrn_decode_kernel(nb, tt, bstep, x_ref, gain_ref, win_ref, lbl_ref, og_ref, wout_ref, fg_ref, s0_ref,
                        y_ref, snew_ref, qe_s, ke_s, dd_s, v_s, oi_s, gs_s, x_s, o_s):
    j = pl.program_id(0)
    rows = nb * tt

    def to_batch_major(dst, val, t):
        for h in range(H_B):
            dst[h, pl.ds(t, nb, stride=tt), :] = val[:, h * LANES:(h + 1) * LANES]

    @pl.when(j == 0)
    def _():
        x = x_ref[...]
        q, kk, v, lg, gs = _hgrn_front(x, gain_ref, win_ref, lbl_ref)
        ones_blk = jnp.ones((DK_B, DK_B), BF16)

        def slab(arr, t):
            return arr[t * nb:(t + 1) * nb, :]

        cum = [slab(lg, 0)]
        for t in range(1, tt):
            cum.append(cum[-1] + slab(lg, t))
        dd_s[...] = jnp.zeros_like(dd_s)
        d = jnp.exp(cum[tt - 1])
        d_hi = d.astype(BF16).astype(F32)
        to_batch_major(dd_s, d_hi, 0)
        to_batch_major(dd_s, d - d_hi, 1)
        for t in range(tt):
            qt = slab(q, t)
            acc = jnp.zeros((nb, INNER_B), F32)
            for s in range(t + 1):
                prod = qt * slab(kk, s)
                if s < t:
                    prod = prod * jnp.exp(cum[t] - cum[s])
                pb = prod.astype(BF16)
                att = jnp.concatenate(
                    [jnp.dot(pb[:, h * DK_B:(h + 1) * DK_B], ones_blk, preferred_element_type=F32)
                     for h in range(H_B)], axis=-1)
                acc = acc + att * slab(v, s)
            to_batch_major(oi_s, acc, t)
            to_batch_major(qe_s, qt * jnp.exp(cum[t]), t)
            to_batch_major(ke_s, slab(kk, t) * jnp.exp(cum[tt - 1] - cum[t]), t)
            to_batch_major(v_s, slab(v, t), t)
            to_batch_major(gs_s, slab(gs, t), t)
            to_batch_major(x_s, slab(x, t), t)

    row1 = lax.broadcasted_iota(jnp.int32, (8, LANES), 0)
    lane2 = lax.broadcasted_iota(jnp.int32, (8, 2 * DV_B), 1)
    row2 = lax.broadcasted_iota(jnp.int32, (8, 2 * DV_B), 0)
    own = (row1 < tt, row1 >= tt)
    ones_rows = tuple(
        jnp.where((lane2 >= DV_B) & (row2 >= lo) & (row2 < lo + 2), 1.0, 0.0).astype(BF16) for lo in (tt, 0))

    def pair_body(pi, carry):
        off = pl.multiple_of((j * (bstep // 2) + pi) * 8, 8)
        for h in range(H_B):
            qe = qe_s[h, pl.ds(off, 8), :].astype(BF16)
            ke = ke_s[h, pl.ds(off, 8), :]
            dd = pltpu.roll(dd_s[h, pl.ds(off, 8), :], tt, axis=0)
            vv = v_s[h, pl.ds(off, 8), :]
            inter = []
            for e in range(2):
                s0 = s0_ref[2 * pi + e, h]
                inter.append(jnp.dot(qe, s0.astype(BF16), preferred_element_type=F32))
                lhs = jnp.where(own[e], ke, dd).astype(BF16)
                ve = jnp.where(own[e], vv, 0.0).astype(BF16)
                rhs = jnp.concatenate([ve, jnp.zeros((8, DV_B), BF16)], axis=-1) + ones_rows[e]
                upd = lax.dot_general(lhs, rhs, (((0,), (0,)), ((), ())),
                                      preferred_element_type=F32)
                snew_ref[2 * pi + e, h] = upd[:, DV_B:] * s0 + upd[:, :DV_B]
            o_s[pl.ds(off, 8), h * DV_B:(h + 1) * DV_B] = (
                jnp.where(own[0], inter[0], inter[1]) + oi_s[h, pl.ds(off, 8), :])
        return carry

    lax.fori_loop(0, bstep // 2, pair_body, 0, unroll=True)

    @pl.when(j == pl.num_programs(0) - 1)
    def _():
        xb = jnp.concatenate([x_s[h] for h in range(H_B)], axis=-1)
        gsb = jnp.concatenate([gs_s[h] for h in range(H_B)], axis=-1)
        y_ref[...] = _hgrn_back(xb, o_s[...], gsb, og_ref, wout_ref, fg_ref)


def _hgrn_decode(x1, s0, gain, w_in, lbl, o_gain, w_out, f_gain, *, nb, tt, bstep):
    assert 2 * tt == 8 and bstep % 2 == 0
    rows = nb * tt
    slab = pltpu.VMEM((H_B, rows, LANES), F32)
    return pl.pallas_call(
        functools.partial(_hgrn_decode_kernel, nb, tt, bstep),
        grid=(nb // bstep,),
        in_specs=[_const_spec((rows, D_MODEL)),
                  _const_spec((1, D_MODEL)),
                  _const_spec((D_MODEL, 4 * INNER_B)),
                  _const_spec((2, INNER_B)),
                  _const_spec((1, INNER_B)),
                  _const_spec((INNER_B, D_MODEL)),
                  _const_spec((1, D_MODEL)),
                  pl.BlockSpec((bstep, H_B, DK_B, DV_B), lambda j: (j, 0, 0, 0))],
        out_specs=[pl.BlockSpec((rows, D_MODEL), lambda j: (0, 0)),
                   pl.BlockSpec((bstep, H_B, DK_B, DV_B), lambda j: (j, 0, 0, 0))],
        out_shape=[jax.ShapeDtypeStruct((rows, D_MODEL), F32),
                   jax.ShapeDtypeStruct((nb, H_B, DK_B, DV_B), F32)],
        scratch_shapes=[slab, slab, slab, slab, slab, slab, slab,
                        pltpu.VMEM((rows, INNER_B), F32)],
        compiler_params=pltpu.CompilerParams(dimension_semantics=("arbitrary",),
                                             vmem_limit_bytes=VMEM_LIMIT),
        name="hgrn_decode",
    )(x1, gain, w_in, lbl, o_gain, w_out, f_gain, s0)


def _gate_weights(w_r, w_i):
    per = GATE_SUPER // BLK_W

    def block_diag(w):
        w4 = w.reshape(N_SUPER, per, BLK_W, BLK_W)
        eye = jnp.eye(per, dtype=w.dtype)
        return jnp.einsum('saij,ac->saicj', w4, eye).reshape(N_SUPER, GATE_SUPER, GATE_SUPER)

    dr, di = block_diag(w_r), block_diag(w_i)
    out = []
    for j, (lo, hi) in enumerate(GATE_K_RANGES):
        cols = slice(j * LANES, (j + 1) * LANES)
        out.append(jnp.concatenate([dr[:, lo:hi, cols], di[:, lo:hi, cols]], axis=-1).astype(BF16))
    return out


def kernel(x_prompt, x_sample, state_lru_h, state_lru_conv, state_hgrn, norm_gain, a_w_in, a_conv_w, a_conv_b,
           a_w_r, a_b_r, a_w_i, a_b_i, a_lambda, a_w_out, b_w_in, b_lb_logits, b_o_gain, b_w_out, final_gain):
    assert norm_gain.shape[0] == 2 and a_w_in.shape[0] == 1 and b_w_in.shape[0] == 1
    pb, pt, _ = x_prompt.shape
    sb, st, _ = x_sample.shape
    halo_t = CONV_W - 1

    row = lambda p: p.reshape(1, -1)
    lru_col_scale = jnp.concatenate([jnp.ones((W_LRU,), F32), jnp.full((W_LRU,), 0.5, F32)])
    hgrn_col_scale = jnp.concatenate([jnp.full((2 * INNER_B,), 0.5, F32), jnp.ones((INNER_B,), F32),
                                      jnp.full((INNER_B,), 0.5, F32)])
    lru_params = (row(norm_gain[0]), (a_w_in[0] * lru_col_scale).astype(BF16), 0.5 * a_conv_w[0],
                  row(0.5 * a_conv_b[0]), *_gate_weights(a_w_r[0], a_w_i[0]), row(0.5 * a_b_r[0]),
                  row(0.5 * a_b_i[0]), row(a_lambda[0]), a_w_out[0].astype(BF16))
    hgrn_params = (row(norm_gain[1]), (b_w_in[0] * hgrn_col_scale).astype(BF16), b_lb_logits, row(b_o_gain[0]),
                   b_w_out[0].astype(BF16), row(final_gain))

    x1p, hp, bufp = _lru_layer(x_prompt, jnp.zeros((pb, W_LRU), F32), jnp.zeros((halo_t * pb, W_LRU), F32),
                               *lru_params, nb=pb, tt=64, batch_major=True)
    y_prompt, sp = _hgrn_prompt(x1p, *hgrn_params, tt=512)
    bufp = bufp.reshape(halo_t, pb, W_LRU).transpose(1, 0, 2)

    xs = x_sample.transpose(1, 0, 2).reshape(st * sb, D_MODEL)
    bufs0 = state_lru_conv[0].transpose(1, 0, 2).reshape(halo_t * sb, W_LRU)
    x1s, hs, bufs = _lru_layer(xs, state_lru_h[0], bufs0, *lru_params, nb=sb, tt=st, batch_major=False)
    ys, ss = _hgrn_decode(x1s, state_hgrn[0], *hgrn_params, nb=sb, tt=st, bstep=8)
    bufs = bufs.reshape(halo_t, sb, W_LRU).transpose(1, 0, 2)

    return (y_prompt, ys.reshape(sb, st, D_MODEL), hp[None], bufp[None], sp[None],
            hs[None], bufs[None], ss[None])
```

```python
import functools

import numpy as np

import jax
import jax.numpy as jnp
from jax import lax
from jax.experimental import pallas as pl
from jax.experimental.pallas import tpu as pltpu

F32 = jnp.float32
BF16 = jnp.bfloat16

D_MODEL = 1024
W_LRU = 1536
N_BLK = 16
BLK_W = W_LRU // N_BLK
CONV_W = 4
LRU_C = 8.0
H_B = 8
DK_B = 128
DV_B = 128
INNER_B = H_B * DK_B
EPS = 1e-6

LANES = 128
GATE_SUPER = 384
N_SUPER = W_LRU // GATE_SUPER
LRU_PARTS = 2
HGRN_PARTS = 2
LRU_TILE_STEPS = 64
HGRN_TILE_ROWS = 512
DECODE_SEQS_PER_STEP = 8
CHUNK = 64
LOG_CHUNK = 6
VMEM_LIMIT = 60 * 1024 * 1024
MAX_LOCAL_EXPONENT = 75.0
LOG2_E = 1.4426950408889634
LOG_FLOOR = -1e4


def _silu_from_half(hx):
    return hx + hx * jnp.tanh(hx)


def _rms_scale(x, gain):
    ms = jnp.mean(x * x, axis=-1, keepdims=True)
    return x * lax.rsqrt(ms + EPS) * gain


def _const_spec(shape):
    zeros = (0,) * len(shape)
    return pl.BlockSpec(shape, lambda *_: zeros, pipeline_mode=pl.Buffered(1))


def _gate_k_ranges():
    ranges = []
    for j in range(GATE_SUPER // LANES):
        first_blk = (j * LANES) // BLK_W
        last_blk = ((j + 1) * LANES - 1) // BLK_W
        lo = (first_blk * BLK_W) // LANES * LANES
        hi = -(-((last_blk + 1) * BLK_W) // LANES) * LANES
        ranges.append((lo, hi))
    return ranges


GATE_K_RANGES = _gate_k_ranges()


def _lru_kernel(nb, tt, batch_major, x_ref, gain_ref, win_ref, cw_ref, cb_ref, wg0_ref, wg1_ref, wg2_ref,
                br_ref, bi_ref, lam_ref, wout_ref, h0_ref, buf0_ref, x1_ref, hlast_ref, bufout_ref, *scratch):
    wg_refs = (wg0_ref, wg1_ref, wg2_ref)
    if batch_major:
        xs_s, os_s, xbuf_s, a_s, b_s, g_s, hc_s = scratch
    else:
        xbuf_s, a_s, b_s, g_s, hc_s = scratch
    rows = nb * tt
    halo = (CONV_W - 1) * nb
    n_lane_tiles = D_MODEL // LANES
    tp = tt // LRU_PARTS
    rp = nb * tp

    @pl.when(pl.program_id(0) == 0)
    def _():
        hc_s[...] = h0_ref[...]
        xbuf_s[0:halo, :] = buf0_ref[...]

    if batch_major:
        for b in range(nb):
            for j in range(n_lane_tiles):
                xs_s[j, pl.ds(b, tt, stride=nb), :] = x_ref[b, :, j * LANES:(j + 1) * LANES]

    def load_x(p):
        if batch_major:
            return jnp.concatenate([xs_s[j, p * rp:(p + 1) * rp, :] for j in range(n_lane_tiles)], axis=-1)
        return x_ref[p * rp:(p + 1) * rp, :]

    nlam = -lam_ref[...]
    softplus = jnp.maximum(nlam, 0.0) + jnp.log1p(jnp.exp(-jnp.abs(nlam)))
    half_rate = (-0.5 * LRU_C) * softplus
    half_cw = 0.5 * cw_ref[...]
    half_cb = 0.5 * cb_ref[...]
    half_br = 0.5 * br_ref[...]
    half_bi = 0.5 * bi_ref[...]

    def pre(p):
        r0 = p * rp
        xn = _rms_scale(load_x(p), gain_ref[...]).astype(BF16)
        u = jnp.dot(xn, win_ref[...], preferred_element_type=F32)
        g_s[r0:r0 + rp, :] = _silu_from_half(u[:, W_LRU:])
        xbuf_s[halo + r0:halo + r0 + rp, :] = u[:, :W_LRU]
        xh = half_cb + half_cw[0:1, :] * xbuf_s[r0:r0 + rp, :]
        for k in range(1, CONV_W):
            xh = xh + half_cw[k:k + 1, :] * xbuf_s[r0 + k * nb:r0 + k * nb + rp, :]
        xhb = xh.astype(BF16)
        r_parts, i_parts = [], []
        for s in range(N_SUPER):
            for (lo, hi), wg_ref in zip(GATE_K_RANGES, wg_refs):
                gsi = jnp.dot(xhb[:, s * GATE_SUPER + lo:s * GATE_SUPER + hi], wg_ref[s],
                              preferred_element_type=F32)
                r_parts.append(gsi[:, :LANES])
                i_parts.append(gsi[:, LANES:])
        tr = jnp.tanh(jnp.concatenate(r_parts, axis=-1) + half_br)
        ti = jnp.tanh(jnp.concatenate(i_parts, axis=-1) + half_bi)
        log_a = half_rate + half_rate * tr
        a = jnp.exp(log_a)
        one_m_a2 = -jnp.tanh(log_a) * (a * a + 1.0)
        a_s[r0:r0 + rp, :] = a
        root = jnp.where(one_m_a2 > 0.0, one_m_a2 * lax.rsqrt(one_m_a2), 0.0)
        b_s[r0:r0 + rp, :] = root * (xh * (1.0 + ti))

    def scan(p, h):
        for t in range(p * tp, (p + 1) * tp):
            h = a_s[t * nb:(t + 1) * nb, :] * h + b_s[t * nb:(t + 1) * nb, :]
            b_s[t * nb:(t + 1) * nb, :] = h
        return h

    def post(p):
        r0 = p * rp
        y = (b_s[r0:r0 + rp, :] * g_s[r0:r0 + rp, :]).astype(BF16)
        out = load_x(p) + jnp.dot(y, wout_ref[...], preferred_element_type=F32)
        if batch_major:
            for j in range(n_lane_tiles):
                os_s[j, r0:r0 + rp, :] = out[:, j * LANES:(j + 1) * LANES]
            for b in range(nb):
                x1_ref[b, p * tp:(p + 1) * tp, :] = jnp.concatenate(
                    [os_s[j, pl.ds(r0 + b, tp, stride=nb), :] for j in range(n_lane_tiles)], axis=-1)
        else:
            x1_ref[r0:r0 + rp, :] = out

    for p in range(LRU_PARTS):
        pre(p)
    new_halo = xbuf_s[rows:rows + halo, :]
    bufout_ref[...] = new_halo
    xbuf_s[0:halo, :] = new_halo
    h = hc_s[...]
    for p in range(LRU_PARTS):
        h = scan(p, h)
        post(p)
    hc_s[...] = h
    hlast_ref[...] = h


def _lru_layer(x, h0, buf0, gain, w_in, conv_w, conv_b, wg0, wg1, wg2, b_r, b_i, lam, w_out, *, nb, tt, batch_major):
    if batch_major:
        total_t = x.shape[1]
        x_spec = pl.BlockSpec((nb, tt, D_MODEL), lambda i: (0, i, 0))
        x_shape = jax.ShapeDtypeStruct((nb, total_t, D_MODEL), F32)
    else:
        total_t = x.shape[0] // nb
        x_spec = pl.BlockSpec((nb * tt, D_MODEL), lambda i: (i, 0))
        x_shape = jax.ShapeDtypeStruct((nb * total_t, D_MODEL), F32)
    rows = nb * tt
    halo = (CONV_W - 1) * nb
    scratch = []
    if batch_major:
        scratch += [pltpu.VMEM((D_MODEL // LANES, rows, LANES), F32),
                    pltpu.VMEM((D_MODEL // LANES, rows, LANES), F32)]
    scratch += [pltpu.VMEM((rows + halo, W_LRU), F32),
                pltpu.VMEM((rows, W_LRU), F32),
                pltpu.VMEM((rows, W_LRU), F32),
                pltpu.VMEM((rows, W_LRU), F32),
                pltpu.VMEM((nb, W_LRU), F32)]
    return pl.pallas_call(
        functools.partial(_lru_kernel, nb, tt, batch_major),
        grid=(total_t // tt,),
        in_specs=[x_spec,
                  _const_spec((1, D_MODEL)),
                  _const_spec((D_MODEL, 2 * W_LRU)),
                  _const_spec((CONV_W, W_LRU)),
                  _const_spec((1, W_LRU)),
                  *[_const_spec((N_SUPER, hi - lo, 2 * LANES)) for lo, hi in GATE_K_RANGES],
                  _const_spec((1, W_LRU)),
                  _const_spec((1, W_LRU)),
                  _const_spec((1, W_LRU)),
                  _const_spec((W_LRU, D_MODEL)),
                  _const_spec((nb, W_LRU)),
                  _const_spec((halo, W_LRU))],
        out_specs=[x_spec,
                   pl.BlockSpec((nb, W_LRU), lambda i: (0, 0)),
                   pl.BlockSpec((halo, W_LRU), lambda i: (0, 0))],
        out_shape=[x_shape,
                   jax.ShapeDtypeStruct((nb, W_LRU), F32),
                   jax.ShapeDtypeStruct((halo, W_LRU), F32)],
        scratch_shapes=scratch,
        compiler_params=pltpu.CompilerParams(dimension_semantics=("arbitrary",),
                                             vmem_limit_bytes=VMEM_LIMIT),
        name="lru_layer_bm" if batch_major else "lru_layer_tm",
    )(x, gain, w_in, conv_w, conv_b, wg0, wg1, wg2, b_r, b_i, lam, w_out, h0, buf0)


def _lower_bound(lbl_ref):
    l0 = lbl_ref[0:1, :]
    l1 = lbl_ref[1:2, :]
    mx = jnp.maximum(l0, l1)
    e0 = jnp.exp(l0 - mx)
    e1 = jnp.exp(l1 - mx)
    return e1 / (e0 + e1)


def _hgrn_front(x, gain_ref, win_ref, lbl_ref):
    xn = _rms_scale(x, gain_ref[...]).astype(BF16)
    n = INNER_B
    q = _silu_from_half(jnp.dot(xn, win_ref[:, 0:n], preferred_element_type=F32))
    tf = jnp.tanh(jnp.dot(xn, win_ref[:, n:2 * n], preferred_element_type=F32))
    v = jnp.dot(xn, win_ref[:, 2 * n:3 * n], preferred_element_type=F32)
    gs = _silu_from_half(jnp.dot(xn, win_ref[:, 3 * n:4 * n], preferred_element_type=F32))
    lb = _lower_bound(lbl_ref)
    c1 = 0.5 * (1.0 - lb)
    ct = c1 * tf
    kk = c1 - ct
    g = (lb + c1) + ct
    lg = jnp.maximum(jnp.log(g), LOG_FLOOR)
    return q, kk, v, lg, gs


def _hgrn_back(x, o, gs, og_ref, wout_ref, fg_ref):
    parts = []
    for h in range(H_B):
        oh = o[:, h * DV_B:(h + 1) * DV_B]
        ms = jnp.mean(oh * oh, axis=-1, keepdims=True)
        parts.append(oh * lax.rsqrt(ms + EPS))
    on = jnp.concatenate(parts, axis=-1) * og_ref[...]
    y = (on * gs).astype(BF16)
    x2 = x + jnp.dot(y, wout_ref[...], preferred_element_type=F32)
    return _rms_scale(x2, fg_ref[...])


def _nt(a, b):
    return lax.dot_general(a, b, (((1,), (1,)), ((), ())), preferred_element_type=F32)


def _tn(a, b):
    return lax.dot_general(a, b, (((0,), (0,)), ((), ())), preferred_element_type=F32)


def _hgrn_prompt_kernel(tt, x_ref, gain_ref, win_ref, lbl_ref, og_ref, wout_ref, fg_ref,
                        y_ref, st_ref, q_s, k_s, v_s, cum_s, gs_s, o_s, st_s, st0_s):
    i = pl.program_id(1)
    n_chunks = tt // CHUNK
    half = CHUNK // 2
    rp = tt // HGRN_PARTS
    chunks_per_part = rp // CHUNK

    @pl.when(i == 0)
    def _():
        st_s[...] = jnp.zeros_like(st_s)

    st0_s[...] = st_s[...]

    row_c = lax.broadcasted_iota(jnp.int32, (CHUNK, CHUNK), 0)
    col_c = lax.broadcasted_iota(jnp.int32, (CHUNK, CHUNK), 1)
    tri = (row_c >= col_c).astype(BF16)
    tri3 = jnp.concatenate([tri, tri, tri], axis=-1)

    def front(p):
        r0 = p * rp
        q, kk, v, lg, gs = _hgrn_front(x_ref[r0:r0 + rp, :], gain_ref, win_ref, lbl_ref)
        q_s[r0:r0 + rp, :] = q
        k_s[r0:r0 + rp, :] = kk
        v_s[r0:r0 + rp, :] = v.astype(BF16)
        gs_s[r0:r0 + rp, :] = gs
        lg = lg * LOG2_E
        h1 = lg.astype(BF16)
        r1 = lg - h1.astype(F32)
        h2 = r1.astype(BF16)
        h3 = (r1 - h2.astype(F32)).astype(BF16)
        worst = jnp.zeros((1, INNER_B), F32)
        for c in range(chunks_per_part):
            rs = slice(c * CHUNK, (c + 1) * CHUNK)
            cum = jnp.dot(tri3, jnp.concatenate([h1[rs], h2[rs], h3[rs]], axis=0), preferred_element_type=F32)
            cum_s[r0 + c * CHUNK:r0 + (c + 1) * CHUNK, :] = cum
            mid = cum[half - 1:half, :]
            worst = jnp.maximum(worst, jnp.maximum(-mid, mid - cum[CHUNK - 1:CHUNK, :]))
        return worst

    def back(p):
        r0 = p * rp
        y_ref[r0:r0 + rp, :] = _hgrn_back(x_ref[r0:r0 + rp, :], o_s[r0:r0 + rp, :], gs_s[r0:r0 + rp, :],
                                           og_ref, wout_ref, fg_ref)

    def head_update(h, base, att, qe_b, ke_b, d):
        sl = slice(h * DK_B, (h + 1) * DK_B)
        vh = v_s[pl.ds(base, CHUNK), sl]
        st = st_s[h]
        o = _nt(qe_b[:, sl], st.astype(BF16)) + jnp.dot(att.astype(BF16), vh, preferred_element_type=F32)
        st_s[h] = d[:, sl] * st + _tn(vh, ke_b[:, sl])
        o_s[pl.ds(base, CHUNK), sl] = o

    def chunk_common(base):
        cu = cum_s[pl.ds(base, CHUNK), :]
        tail = cum_s[pl.ds(base + CHUNK - 8, 8), :]
        clast = tail[7:8, :]
        qf = q_s[pl.ds(base, CHUNK), :]
        kf = k_s[pl.ds(base, CHUNK), :]
        qe = qf * jnp.exp2(cu)
        ke_b = (kf * jnp.exp2(clast - cu)).astype(BF16)
        return cu, qf, kf, qe, ke_b, jnp.exp2(clast)

    def fast_chunk(base):
        cu_l = cum_s[pl.ds(base, half), :]
        cu_r = cum_s[pl.ds(base + half, half), :]
        q_l, q_r = q_s[pl.ds(base, half), :], q_s[pl.ds(base + half, half), :]
        k_l, k_r = k_s[pl.ds(base, half), :], k_s[pl.ds(base + half, half), :]
        cmid = cu_l[half - 1:half, :]
        clast = cu_r[half - 1:half, :]
        bf = lambda z: z.astype(BF16)
        qe_l, qe_r = bf(q_l * jnp.exp2(cu_l)), bf(q_r * jnp.exp2(cu_r))
        ke_l, ke_r = bf(k_l * jnp.exp2(clast - cu_l)), bf(k_r * jnp.exp2(clast - cu_r))
        za_l = bf(k_l * jnp.exp2(cmid - cu_l))
        za_r = bf(q_r * jnp.exp2(cu_r - cmid))
        kl_l = bf(k_l * jnp.exp2(-cu_l))
        kl_r = bf(k_r * jnp.exp2(cmid - cu_r))
        qe_b = jnp.concatenate([qe_l, qe_r], axis=0)
        ke_b = jnp.concatenate([ke_l, ke_r], axis=0)
        za_b = jnp.concatenate([za_l, za_r], axis=0)
        kl_b = jnp.concatenate([kl_l, kl_r], axis=0)
        ql_b = jnp.concatenate([qe_l, za_r], axis=0)
        d = jnp.exp2(clast)
        cross = (row_c >= half) & (col_c < half)
        local = (row_c >= col_c) & ((row_c >= half) == (col_c >= half))
        for h in range(H_B):
            sl = slice(h * DK_B, (h + 1) * DK_B)
            pa = _nt(za_b[:, sl], za_b[:, sl])
            pb = _nt(ql_b[:, sl], kl_b[:, sl])
            att = jnp.where(cross, pa, jnp.where(local, pb, 0.0))
            head_update(h, base, att, qe_b, ke_b, d)

    def slow_chunk(c, carry):
        base = pl.multiple_of(c * CHUNK, CHUNK)
        cu_all, qf, kf, qe, ke_b, d = chunk_common(base)
        qe_b = qe.astype(BF16)
        xor_c = row_c ^ col_c
        row_l = lax.broadcasted_iota(jnp.int32, (CHUNK, LANES), 0)
        sub8 = lax.broadcasted_iota(jnp.int32, (8, LANES), 0)
        for h in range(H_B):
            sl = slice(h * DK_B, (h + 1) * DK_B)
            cu = cu_all[:, sl]
            qh = qf[:, sl]
            kh = kf[:, sl]

            def ref_row(r):
                grp = cu[(r // 8) * 8:(r // 8) * 8 + 8, :]
                return grp[r % 8:r % 8 + 1, :]

            att = jnp.zeros((CHUNK, CHUNK), F32)
            for lm in range(LOG_CHUNK):
                m = 1 << lm
                if lm >= 2:
                    cm = jnp.concatenate(
                        [jnp.broadcast_to(ref_row(blk * 2 * m + m - 1), (2 * m, LANES))
                         for blk in range(CHUNK // (2 * m))], axis=0)
                elif lm == 1:
                    cm = jnp.concatenate(
                        [jnp.where(sub8 < 4,
                                   jnp.broadcast_to(ref_row(blk * 8 + 1), (8, LANES)),
                                   jnp.broadcast_to(ref_row(blk * 8 + 5), (8, LANES)))
                         for blk in range(CHUNK // 8)], axis=0)
                else:
                    cm = jnp.where((row_l & 1) == 1, pltpu.roll(cu, 1, axis=0), cu)
                right = ((row_l >> lm) & 1) == 1
                z = (jnp.where(right, qh, kh) * jnp.exp2(-jnp.abs(cu - cm))).astype(BF16)
                pair = ((xor_c >> lm) == 1) & (((row_c >> lm) & 1) == 1)
                att = jnp.where(pair, _nt(z, z), att)
            att = jnp.where(row_c == col_c, _nt(qh.astype(BF16), kh.astype(BF16)), att)
            head_update(h, base, att, qe_b, ke_b, d)
        return carry

    worst = front(0)
    for p in range(1, HGRN_PARTS):
        worst = jnp.maximum(worst, front(p))
    for c in range(n_chunks):
        fast_chunk(c * CHUNK)
    for p in range(HGRN_PARTS):
        back(p)

    @pl.when(jnp.max(worst) > MAX_LOCAL_EXPONENT * LOG2_E)
    def _():
        st_s[...] = st0_s[...]
        lax.fori_loop(0, n_chunks, slow_chunk, 0)
        for p in range(HGRN_PARTS):
            back(p)

    @pl.when(i == pl.num_programs(1) - 1)
    def _():
        for h in range(H_B):
            st_ref[h] = st_s[h].T


def _hgrn_prompt(x1, gain, w_in, lbl, o_gain, w_out, f_gain, *, tt):
    nb, total_t, _ = x1.shape
    return pl.pallas_call(
        functools.partial(_hgrn_prompt_kernel, tt),
        grid=(nb, total_t // tt),
        in_specs=[pl.BlockSpec((None, tt, D_MODEL), lambda b, i: (b, i, 0)),
                  _const_spec((1, D_MODEL)),
                  _const_spec((D_MODEL, 4 * INNER_B)),
                  _const_spec((2, INNER_B)),
                  _const_spec((1, INNER_B)),
                  _const_spec((INNER_B, D_MODEL)),
                  _const_spec((1, D_MODEL))],
        out_specs=[pl.BlockSpec((None, tt, D_MODEL), lambda b, i: (b, i, 0)),
                   pl.BlockSpec((None, H_B, DK_B, DV_B), lambda b, i: (b, 0, 0, 0))],
        out_shape=[jax.ShapeDtypeStruct((nb, total_t, D_MODEL), F32),
                   jax.ShapeDtypeStruct((nb, H_B, DK_B, DV_B), F32)],
        scratch_shapes=[pltpu.VMEM((tt, INNER_B), F32),
                        pltpu.VMEM((tt, INNER_B), F32),
                        pltpu.VMEM((tt, INNER_B), BF16),
                        pltpu.VMEM((tt, INNER_B), F32),
                        pltpu.VMEM((tt, INNER_B), F32),
                        pltpu.VMEM((tt, INNER_B), F32),
                        pltpu.VMEM((H_B, DV_B, DK_B), F32),
                        pltpu.VMEM((H_B, DV_B, DK_B), F32)],
        compiler_params=pltpu.CompilerParams(dimension_semantics=("arbitrary", "arbitrary"),
                                             vmem_limit_bytes=VMEM_LIMIT),
        name="hgrn_prompt",
    )(x1, gain, w_in, lbl, o_gain, w_out, f_gain)


def _hgrn_decode_kernel(nb, tt, bstep, x_ref, gain_ref, win_ref, lbl_ref, og_ref, wout_ref, fg_ref, s0_ref,
                        y_ref, snew_ref, qe_s, ke_s, dd_s, v_s, oi_s, gs_s, x_s, o_s):
    j = pl.program_id(0)
    rows = nb * tt

    def to_batch_major(dst, val, t):
        for h in range(H_B):
            dst[h, pl.ds(t, nb, stride=tt), :] = val[:, h * LANES:(h + 1) * LANES]

    @pl.when(j == 0)
    def _():
        x = x_ref[...]
        q, kk, v, lg, gs = _hgrn_front(x, gain_ref, win_ref, lbl_ref)
        ones_blk = jnp.ones((DK_B, DK_B), BF16)

        def slab(arr, t):
            return arr[t * nb:(t + 1) * nb, :]

        cum = [slab(lg, 0)]
        for t in range(1, tt):
            cum.append(cum[-1] + slab(lg, t))
        dd_s[...] = jnp.zeros_like(dd_s)
        d = jnp.exp(cum[tt - 1])
        d_hi = d.astype(BF16).astype(F32)
        to_batch_major(dd_s, d_hi, 0)
        to_batch_major(dd_s, d - d_hi, 1)
        for t in range(tt):
            qt = slab(q, t)
            acc = jnp.zeros((nb, INNER_B), F32)
            for s in range(t + 1):
                prod = qt * slab(kk, s)
                if s < t:
                    prod = prod * jnp.exp(cum[t] - cum[s])
                pb = prod.astype(BF16)
                att = jnp.concatenate(
                    [jnp.dot(pb[:, h * DK_B:(h + 1) * DK_B], ones_blk, preferred_element_type=F32)
                     for h in range(H_B)], axis=-1)
                acc = acc + att * slab(v, s)
            to_batch_major(oi_s, acc, t)
            to_batch_major(qe_s, qt * jnp.exp(cum[t]), t)
            to_batch_major(ke_s, slab(kk, t) * jnp.exp(cum[tt - 1] - cum[t]), t)
            to_batch_major(v_s, slab(v, t), t)
            to_batch_major(gs_s, slab(gs, t), t)
            to_batch_major(x_s, slab(x, t), t)

    row1 = lax.broadcasted_iota(jnp.int32, (8, LANES), 0)
    lane2 = lax.broadcasted_iota(jnp.int32, (8, 2 * DV_B), 1)
    row2 = lax.broadcasted_iota(jnp.int32, (8, 2 * DV_B), 0)
    own = (row1 < tt, row1 >= tt)
    ones_rows = tuple(
        jnp.where((lane2 >= DV_B) & (row2 >= lo) & (row2 < lo + 2), 1.0, 0.0).astype(BF16) for lo in (tt, 0))

    def pair_body(pi, carry):
        off = pl.multiple_of((j * (bstep // 2) + pi) * 8, 8)
        for h in range(H_B):
            qe = qe_s[h, pl.ds(off, 8), :].astype(BF16)
            ke = ke_s[h, pl.ds(off, 8), :]
            dd = pltpu.roll(dd_s[h, pl.ds(off, 8), :], tt, axis=0)
            vv = v_s[h, pl.ds(off, 8), :]
            inter = []
            for e in range(2):
                s0 = s0_ref[2 * pi + e, h]
                inter.append(jnp.dot(qe, s0.astype(BF16), preferred_element_type=F32))
                lhs = jnp.where(own[e], ke, dd).astype(BF16)
                ve = jnp.where(own[e], vv, 0.0).astype(BF16)
                rhs = jnp.concatenate([ve, jnp.zeros((8, DV_B), BF16)], axis=-1) + ones_rows[e]
                upd = lax.dot_general(lhs, rhs, (((0,), (0,)), ((), ())),
                                      preferred_element_type=F32)
                snew_ref[2 * pi + e, h] = upd[:, DV_B:] * s0 + upd[:, :DV_B]
            o_s[pl.ds(off, 8), h * DV_B:(h + 1) * DV_B] = (
                jnp.where(own[0], inter[0], inter[1]) + oi_s[h, pl.ds(off, 8), :])
        return carry

    lax.fori_loop(0, bstep // 2, pair_body, 0, unroll=True)

    @pl.when(j == pl.num_programs(0) - 1)
    def _():
        xb = jnp.concatenate([x_s[h] for h in range(H_B)], axis=-1)
        gsb = jnp.concatenate([gs_s[h] for h in range(H_B)], axis=-1)
        y_ref[...] = _hgrn_back(xb, o_s[...], gsb, og_ref, wout_ref, fg_ref)


def _hgrn_decode(x1, s0, gain, w_in, lbl, o_gain, w_out, f_gain, *, nb, tt, bstep):
    assert 2 * tt == 8 and bstep % 2 == 0
    rows = nb * tt
    slab = pltpu.VMEM((H_B, rows, LANES), F32)
    return pl.pallas_call(
        functools.partial(_hgrn_decode_kernel, nb, tt, bstep),
        grid=(nb // bstep,),
        in_specs=[_const_spec((rows, D_MODEL)),
                  _const_spec((1, D_MODEL)),
                  _const_spec((D_MODEL, 4 * INNER_B)),
                  _const_spec((2, INNER_B)),
                  _const_spec((1, INNER_B)),
                  _const_spec((INNER_B, D_MODEL)),
                  _const_spec((1, D_MODEL)),
                  pl.BlockSpec((bstep, H_B, DK_B, DV_B), lambda j: (j, 0, 0, 0))],
        out_specs=[pl.BlockSpec((rows, D_MODEL), lambda j: (0, 0)),
                   pl.BlockSpec((bstep, H_B, DK_B, DV_B), lambda j: (j, 0, 0, 0))],
        out_shape=[jax.ShapeDtypeStruct((rows, D_MODEL), F32),
                   jax.ShapeDtypeStruct((nb, H_B, DK_B, DV_B), F32)],
        scratch_shapes=[slab, slab, slab, slab, slab, slab, slab,
                        pltpu.VMEM((rows, INNER_B), F32)],
        compiler_params=pltpu.CompilerParams(dimension_semantics=("arbitrary",),
                                             vmem_limit_bytes=VMEM_LIMIT),
        name="hgrn_decode",
    )(x1, gain, w_in, lbl, o_gain, w_out, f_gain, s0)


def _gate_weights(w_r, w_i):
    per = GATE_SUPER // BLK_W

    def block_diag(w):
        w4 = w.reshape(N_SUPER, per, BLK_W, BLK_W)
        eye = jnp.eye(per, dtype=w.dtype)
        return jnp.einsum('saij,ac->saicj', w4, eye).reshape(N_SUPER, GATE_SUPER, GATE_SUPER)

    dr, di = block_diag(w_r), block_diag(w_i)
    out = []
    for j, (lo, hi) in enumerate(GATE_K_RANGES):
        cols = slice(j * LANES, (j + 1) * LANES)
        out.append(jnp.concatenate([dr[:, lo:hi, cols], di[:, lo:hi, cols]], axis=-1).astype(BF16))
    return out


def kernel(x_prompt, x_sample, state_lru_h, state_lru_conv, state_hgrn, norm_gain, a_w_in, a_conv_w, a_conv_b,
           a_w_r, a_b_r, a_w_i, a_b_i, a_lambda, a_w_out, b_w_in, b_lb_logits, b_o_gain, b_w_out, final_gain):
    assert norm_gain.shape[0] == 2 and a_w_in.shape[0] == 1 and b_w_in.shape[0] == 1
    pb, pt, _ = x_prompt.shape
    sb, st, _ = x_sample.shape
    halo_t = CONV_W - 1

    row = lambda p: p.reshape(1, -1)
    lru_col_scale = np.concatenate([np.ones((W_LRU,), np.float32), np.full((W_LRU,), 0.5, np.float32)])
    hgrn_col_scale = np.concatenate([np.full((2 * INNER_B,), 0.5, np.float32), np.ones((INNER_B,), np.float32),
                                     np.full((INNER_B,), 0.5, np.float32)])
    lru_params = (row(norm_gain[0]), (a_w_in[0] * lru_col_scale).astype(BF16), a_conv_w[0],
                  row(a_conv_b[0]), *_gate_weights(a_w_r[0], a_w_i[0]), row(a_b_r[0]),
                  row(a_b_i[0]), row(a_lambda[0]), a_w_out[0].astype(BF16))
    hgrn_params = (row(norm_gain[1]), (b_w_in[0] * hgrn_col_scale).astype(BF16), b_lb_logits, row(b_o_gain[0]),
                   b_w_out[0].astype(BF16), row(final_gain))

    x1p, hp, bufp = _lru_layer(x_prompt, jnp.zeros((pb, W_LRU), F32), jnp.zeros((halo_t * pb, W_LRU), F32),
                               *lru_params, nb=pb, tt=LRU_TILE_STEPS, batch_major=True)
    y_prompt, sp = _hgrn_prompt(x1p, *hgrn_params, tt=HGRN_TILE_ROWS)
    bufp = bufp.reshape(halo_t, pb, W_LRU).transpose(1, 0, 2)

    xs = x_sample.transpose(1, 0, 2).reshape(st * sb, D_MODEL)
    bufs0 = state_lru_conv[0].transpose(1, 0, 2).reshape(halo_t * sb, W_LRU)
    x1s, hs, bufs = _lru_layer(xs, state_lru_h[0], bufs0, *lru_params, nb=sb, tt=st, batch_major=False)
    ys, ss = _hgrn_decode(x1s, state_hgrn[0], *hgrn_params, nb=sb, tt=st, bstep=DECODE_SEQS_PER_STEP)
    bufs = bufs.reshape(halo_t, sb, W_LRU).transpose(1, 0, 2)

    return (y_prompt, ys.reshape(sb, st, D_MODEL), hp[None], bufp[None], sp[None],
            hs[None], bufs[None], ss[None])
```

```python
import functools

import numpy as np

import jax
import jax.numpy as jnp
from jax import lax
from jax.experimental import pallas as pl
from jax.experimental.pallas import tpu as pltpu

F32 = jnp.float32
BF16 = jnp.bfloat16

D_MODEL = 1024
W_LRU = 1536
N_BLK = 16
BLK_W = W_LRU // N_BLK
CONV_W = 4
LRU_C = 8.0
H_B = 8
DK_B = 128
DV_B = 128
INNER_B = H_B * DK_B
EPS = 1e-6

LANES = 128
GATE_SUPER = 384
N_SUPER = W_LRU // GATE_SUPER
LRU_PARTS = 1
HGRN_PARTS = 2
LRU_CHANNEL_BLOCK = 768
LRU_TILE_STEPS = 64
HGRN_TILE_ROWS = 512
DECODE_SEQS_PER_STEP = 8
CHUNK = 64
LOG_CHUNK = 6
VMEM_LIMIT = 60 * 1024 * 1024
MAX_LOCAL_EXPONENT = 75.0
LOG2_E = 1.4426950408889634
LOG_FLOOR = -1e4


def _silu_from_half(hx):
    return hx + hx * jnp.tanh(hx)


def _rms_scale(x, gain):
    ms = jnp.mean(x * x, axis=-1, keepdims=True)
    return x * lax.rsqrt(ms + EPS) * gain


def _const_spec(shape):
    zeros = (0,) * len(shape)
    return pl.BlockSpec(shape, lambda *_: zeros, pipeline_mode=pl.Buffered(1))


def _gate_k_ranges():
    ranges = []
    for j in range(GATE_SUPER // LANES):
        first_blk = (j * LANES) // BLK_W
        last_blk = ((j + 1) * LANES - 1) // BLK_W
        lo = (first_blk * BLK_W) // LANES * LANES
        hi = -(-((last_blk + 1) * BLK_W) // LANES) * LANES
        ranges.append((lo, hi))
    return ranges


GATE_K_RANGES = _gate_k_ranges()


def _lru_kernel(nb, tt, batch_major, x_ref, gain_ref, win_ref, cw_ref, cb_ref, wg0_ref, wg1_ref, wg2_ref,
                br_ref, bi_ref, lam_ref, wout_ref, h0_ref, buf0_ref, x1_ref, hlast_ref, bufout_ref, *scratch):
    wg_refs = (wg0_ref, wg1_ref, wg2_ref)
    if batch_major:
        xs_s, os_s, xbuf_s, a_s, b_s, g_s, hc_s = scratch
    else:
        xbuf_s, a_s, b_s, g_s, hc_s = scratch
    rows = nb * tt
    halo = (CONV_W - 1) * nb
    n_lane_tiles = D_MODEL // LANES
    tp = tt // LRU_PARTS
    rp = nb * tp

    @pl.when(pl.program_id(0) == 0)
    def _():
        hc_s[...] = h0_ref[...]
        xbuf_s[0:halo, :] = buf0_ref[...]

    if batch_major:
        for b in range(nb):
            for j in range(n_lane_tiles):
                xs_s[j, pl.ds(b, tt, stride=nb), :] = x_ref[b, :, j * LANES:(j + 1) * LANES]

    def load_x(p):
        if batch_major:
            return jnp.concatenate([xs_s[j, p * rp:(p + 1) * rp, :] for j in range(n_lane_tiles)], axis=-1)
        return x_ref[p * rp:(p + 1) * rp, :]

    nlam = -lam_ref[...]
    softplus = jnp.maximum(nlam, 0.0) + jnp.log1p(jnp.exp(-jnp.abs(nlam)))
    half_rate = (-0.5 * LRU_C) * softplus
    half_cw = 0.5 * cw_ref[...]
    half_cb = 0.5 * cb_ref[...]
    half_br = 0.5 * br_ref[...]
    half_bi = 0.5 * bi_ref[...]

    def pre(p):
        r0 = p * rp
        xn = _rms_scale(load_x(p), gain_ref[...]).astype(BF16)
        for c0 in range(0, W_LRU, LRU_CHANNEL_BLOCK):
            cs = slice(c0, c0 + LRU_CHANNEL_BLOCK)
            xbuf_s[halo + r0:halo + r0 + rp, cs] = jnp.dot(xn, win_ref[:, cs], preferred_element_type=F32)
            xh = half_cb[:, cs] + half_cw[0:1, cs] * xbuf_s[r0:r0 + rp, cs]
            for k in range(1, CONV_W):
                xh = xh + half_cw[k:k + 1, cs] * xbuf_s[r0 + k * nb:r0 + k * nb + rp, cs]
            xhb = xh.astype(BF16)
            r_parts, i_parts = [], []
            for s in range(LRU_CHANNEL_BLOCK // GATE_SUPER):
                for (lo, hi), wg_ref in zip(GATE_K_RANGES, wg_refs):
                    gsi = jnp.dot(xhb[:, s * GATE_SUPER + lo:s * GATE_SUPER + hi],
                                  wg_ref[c0 // GATE_SUPER + s], preferred_element_type=F32)
                    r_parts.append(gsi[:, :LANES])
                    i_parts.append(gsi[:, LANES:])
            tr = jnp.tanh(jnp.concatenate(r_parts, axis=-1) + half_br[:, cs])
            ti = jnp.tanh(jnp.concatenate(i_parts, axis=-1) + half_bi[:, cs])
            log_a = half_rate[:, cs] + half_rate[:, cs] * tr
            a = jnp.exp(log_a)
            one_m_a2 = -jnp.tanh(log_a) * (a * a + 1.0)
            a_s[r0:r0 + rp, cs] = a
            root = jnp.where(one_m_a2 > 0.0, one_m_a2 * lax.rsqrt(one_m_a2), 0.0)
            b_s[r0:r0 + rp, cs] = root * (xh * (1.0 + ti))
            hg = jnp.dot(xn, win_ref[:, W_LRU + c0:W_LRU + c0 + LRU_CHANNEL_BLOCK], preferred_element_type=F32)
            g_s[r0:r0 + rp, cs] = _silu_from_half(hg)

    def scan(p, h):
        for t in range(p * tp, (p + 1) * tp):
            h = a_s[t * nb:(t + 1) * nb, :] * h + b_s[t * nb:(t + 1) * nb, :]
            b_s[t * nb:(t + 1) * nb, :] = h
        return h

    def post(p):
        r0 = p * rp
        y = (b_s[r0:r0 + rp, :] * g_s[r0:r0 + rp, :]).astype(BF16)
        out = load_x(p) + jnp.dot(y, wout_ref[...], preferred_element_type=F32)
        if batch_major:
            for j in range(n_lane_tiles):
                os_s[j, r0:r0 + rp, :] = out[:, j * LANES:(j + 1) * LANES]
            for b in range(nb):
                x1_ref[b, p * tp:(p + 1) * tp, :] = jnp.concatenate(
                    [os_s[j, pl.ds(r0 + b, tp, stride=nb), :] for j in range(n_lane_tiles)], axis=-1)
        else:
            x1_ref[r0:r0 + rp, :] = out

    for p in range(LRU_PARTS):
        pre(p)
    new_halo = xbuf_s[rows:rows + halo, :]
    bufout_ref[...] = new_halo
    xbuf_s[0:halo, :] = new_halo
    h = hc_s[...]
    for p in range(LRU_PARTS):
        h = scan(p, h)
        post(p)
    hc_s[...] = h
    hlast_ref[...] = h


def _lru_layer(x, h0, buf0, gain, w_in, conv_w, conv_b, wg0, wg1, wg2, b_r, b_i, lam, w_out, *, nb, tt, batch_major):
    if batch_major:
        total_t = x.shape[1]
        x_spec = pl.BlockSpec((nb, tt, D_MODEL), lambda i: (0, i, 0))
        x_shape = jax.ShapeDtypeStruct((nb, total_t, D_MODEL), F32)
    else:
        total_t = x.shape[0] // nb
        x_spec = pl.BlockSpec((nb * tt, D_MODEL), lambda i: (i, 0))
        x_shape = jax.ShapeDtypeStruct((nb * total_t, D_MODEL), F32)
    rows = nb * tt
    halo = (CONV_W - 1) * nb
    scratch = []
    if batch_major:
        scratch += [pltpu.VMEM((D_MODEL // LANES, rows, LANES), F32),
                    pltpu.VMEM((D_MODEL // LANES, rows, LANES), F32)]
    scratch += [pltpu.VMEM((rows + halo, W_LRU), F32),
                pltpu.VMEM((rows, W_LRU), F32),
                pltpu.VMEM((rows, W_LRU), F32),
                pltpu.VMEM((rows, W_LRU), F32),
                pltpu.VMEM((nb, W_LRU), F32)]
    return pl.pallas_call(
        functools.partial(_lru_kernel, nb, tt, batch_major),
        grid=(total_t // tt,),
        in_specs=[x_spec,
                  _const_spec((1, D_MODEL)),
                  _const_spec((D_MODEL, 2 * W_LRU)),
                  _const_spec((CONV_W, W_LRU)),
                  _const_spec((1, W_LRU)),
                  *[_const_spec((N_SUPER, hi - lo, 2 * LANES)) for lo, hi in GATE_K_RANGES],
                  _const_spec((1, W_LRU)),
                  _const_spec((1, W_LRU)),
                  _const_spec((1, W_LRU)),
                  _const_spec((W_LRU, D_MODEL)),
                  _const_spec((nb, W_LRU)),
                  _const_spec((halo, W_LRU))],
        out_specs=[x_spec,
                   pl.BlockSpec((nb, W_LRU), lambda i: (0, 0)),
                   pl.BlockSpec((halo, W_LRU), lambda i: (0, 0))],
        out_shape=[x_shape,
                   jax.ShapeDtypeStruct((nb, W_LRU), F32),
                   jax.ShapeDtypeStruct((halo, W_LRU), F32)],
        scratch_shapes=scratch,
        compiler_params=pltpu.CompilerParams(dimension_semantics=("arbitrary",),
                                             vmem_limit_bytes=VMEM_LIMIT),
        name="lru_layer_bm" if batch_major else "lru_layer_tm",
    )(x, gain, w_in, conv_w, conv_b, wg0, wg1, wg2, b_r, b_i, lam, w_out, h0, buf0)


def _lower_bound(lbl_ref):
    l0 = lbl_ref[0:1, :]
    l1 = lbl_ref[1:2, :]
    mx = jnp.maximum(l0, l1)
    e0 = jnp.exp(l0 - mx)
    e1 = jnp.exp(l1 - mx)
    return e1 / (e0 + e1)


def _hgrn_front(x, gain_ref, win_ref, lbl_ref):
    xn = _rms_scale(x, gain_ref[...]).astype(BF16)
    n = INNER_B
    q = _silu_from_half(jnp.dot(xn, win_ref[:, 0:n], preferred_element_type=F32))
    tf = jnp.tanh(jnp.dot(xn, win_ref[:, n:2 * n], preferred_element_type=F32))
    v = jnp.dot(xn, win_ref[:, 2 * n:3 * n], preferred_element_type=F32)
    gs = _silu_from_half(jnp.dot(xn, win_ref[:, 3 * n:4 * n], preferred_element_type=F32))
    lb = _lower_bound(lbl_ref)
    c1 = 0.5 * (1.0 - lb)
    ct = c1 * tf
    kk = c1 - ct
    g = (lb + c1) + ct
    lg = jnp.maximum(jnp.log(g), LOG_FLOOR)
    return q, kk, v, lg, gs


def _hgrn_back(x, o, gs, og_ref, wout_ref, fg_ref):
    parts = []
    for h in range(H_B):
        oh = o[:, h * DV_B:(h + 1) * DV_B]
        ms = jnp.mean(oh * oh, axis=-1, keepdims=True)
        parts.append(oh * lax.rsqrt(ms + EPS))
    on = jnp.concatenate(parts, axis=-1) * og_ref[...]
    y = (on * gs).astype(BF16)
    x2 = x + jnp.dot(y, wout_ref[...], preferred_element_type=F32)
    return _rms_scale(x2, fg_ref[...])


def _nt(a, b):
    return lax.dot_general(a, b, (((1,), (1,)), ((), ())), preferred_element_type=F32)


def _tn(a, b):
    return lax.dot_general(a, b, (((0,), (0,)), ((), ())), preferred_element_type=F32)


def _hgrn_prompt_kernel(tt, x_ref, gain_ref, win_ref, lbl_ref, og_ref, wout_ref, fg_ref,
                        y_ref, st_ref, q_s, k_s, v_s, cum_s, gs_s, o_s, st_s, st0_s):
    i = pl.program_id(1)
    n_chunks = tt // CHUNK
    half = CHUNK // 2
    rp = tt // HGRN_PARTS
    chunks_per_part = rp // CHUNK

    @pl.when(i == 0)
    def _():
        st_s[...] = jnp.zeros_like(st_s)

    st0_s[...] = st_s[...]

    row_c = lax.broadcasted_iota(jnp.int32, (CHUNK, CHUNK), 0)
    col_c = lax.broadcasted_iota(jnp.int32, (CHUNK, CHUNK), 1)
    tri = (row_c >= col_c).astype(BF16)
    tri3 = jnp.concatenate([tri, tri, tri], axis=-1)

    def front(p):
        r0 = p * rp
        q, kk, v, lg, gs = _hgrn_front(x_ref[r0:r0 + rp, :], gain_ref, win_ref, lbl_ref)
        q_s[r0:r0 + rp, :] = q
        k_s[r0:r0 + rp, :] = kk
        v_s[r0:r0 + rp, :] = v.astype(BF16)
        gs_s[r0:r0 + rp, :] = gs
        lg = lg * LOG2_E
        h1 = lg.astype(BF16)
        r1 = lg - h1.astype(F32)
        h2 = r1.astype(BF16)
        h3 = (r1 - h2.astype(F32)).astype(BF16)
        worst = jnp.zeros((1, INNER_B), F32)
        for c in range(chunks_per_part):
            rs = slice(c * CHUNK, (c + 1) * CHUNK)
            cum = jnp.dot(tri3, jnp.concatenate([h1[rs], h2[rs], h3[rs]], axis=0), preferred_element_type=F32)
            cum_s[r0 + c * CHUNK:r0 + (c + 1) * CHUNK, :] = cum
            mid = cum[half - 1:half, :]
            worst = jnp.maximum(worst, jnp.maximum(-mid, mid - cum[CHUNK - 1:CHUNK, :]))
        return worst

    def back(p):
        r0 = p * rp
        y_ref[r0:r0 + rp, :] = _hgrn_back(x_ref[r0:r0 + rp, :], o_s[r0:r0 + rp, :], gs_s[r0:r0 + rp, :],
                                           og_ref, wout_ref, fg_ref)

    def head_update(h, base, att, qe_b, ke_b, d):
        sl = slice(h * DK_B, (h + 1) * DK_B)
        vh = v_s[pl.ds(base, CHUNK), sl]
        st = st_s[h]
        o = _nt(qe_b[:, sl], st.astype(BF16)) + jnp.dot(att.astype(BF16), vh, preferred_element_type=F32)
        st_s[h] = d[:, sl] * st + _tn(vh, ke_b[:, sl])
        o_s[pl.ds(base, CHUNK), sl] = o

    def chunk_common(base):
        cu = cum_s[pl.ds(base, CHUNK), :]
        tail = cum_s[pl.ds(base + CHUNK - 8, 8), :]
        clast = tail[7:8, :]
        qf = q_s[pl.ds(base, CHUNK), :]
        kf = k_s[pl.ds(base, CHUNK), :]
        qe = qf * jnp.exp2(cu)
        ke_b = (kf * jnp.exp2(clast - cu)).astype(BF16)
        return cu, qf, kf, qe, ke_b, jnp.exp2(clast)

    def fast_chunk(base):
        cu_l = cum_s[pl.ds(base, half), :]
        cu_r = cum_s[pl.ds(base + half, half), :]
        q_l, q_r = q_s[pl.ds(base, half), :], q_s[pl.ds(base + half, half), :]
        k_l, k_r = k_s[pl.ds(base, half), :], k_s[pl.ds(base + half, half), :]
        cmid = cu_l[half - 1:half, :]
        clast = cu_r[half - 1:half, :]
        bf = lambda z: z.astype(BF16)
        qe_l, qe_r = bf(q_l * jnp.exp2(cu_l)), bf(q_r * jnp.exp2(cu_r))
        ke_l, ke_r = bf(k_l * jnp.exp2(clast - cu_l)), bf(k_r * jnp.exp2(clast - cu_r))
        za_l = bf(k_l * jnp.exp2(cmid - cu_l))
        za_r = bf(q_r * jnp.exp2(cu_r - cmid))
        kl_l = bf(k_l * jnp.exp2(-cu_l))
        kl_r = bf(k_r * jnp.exp2(cmid - cu_r))
        qe_b = jnp.concatenate([qe_l, qe_r], axis=0)
        ke_b = jnp.concatenate([ke_l, ke_r], axis=0)
        za_b = jnp.concatenate([za_l, za_r], axis=0)
        kl_b = jnp.concatenate([kl_l, kl_r], axis=0)
        ql_b = jnp.concatenate([qe_l, za_r], axis=0)
        d = jnp.exp2(clast)
        cross = (row_c >= half) & (col_c < half)
        local = (row_c >= col_c) & ((row_c >= half) == (col_c >= half))
        for h in range(H_B):
            sl = slice(h * DK_B, (h + 1) * DK_B)
            pa = _nt(za_b[:, sl], za_b[:, sl])
            pb = _nt(ql_b[:, sl], kl_b[:, sl])
            att = jnp.where(cross, pa, jnp.where(local, pb, 0.0))
            head_update(h, base, att, qe_b, ke_b, d)

    def slow_chunk(c, carry):
        base = pl.multiple_of(c * CHUNK, CHUNK)
        cu_all, qf, kf, qe, ke_b, d = chunk_common(base)
        qe_b = qe.astype(BF16)
        xor_c = row_c ^ col_c
        row_l = lax.broadcasted_iota(jnp.int32, (CHUNK, LANES), 0)
        sub8 = lax.broadcasted_iota(jnp.int32, (8, LANES), 0)
        for h in range(H_B):
            sl = slice(h * DK_B, (h + 1) * DK_B)
            cu = cu_all[:, sl]
            qh = qf[:, sl]
            kh = kf[:, sl]

            def ref_row(r):
                grp = cu[(r // 8) * 8:(r // 8) * 8 + 8, :]
                return grp[r % 8:r % 8 + 1, :]

            att = jnp.zeros((CHUNK, CHUNK), F32)
            for lm in range(LOG_CHUNK):
                m = 1 << lm
                if lm >= 2:
                    cm = jnp.concatenate(
                        [jnp.broadcast_to(ref_row(blk * 2 * m + m - 1), (2 * m, LANES))
                         for blk in range(CHUNK // (2 * m))], axis=0)
                elif lm == 1:
                    cm = jnp.concatenate(
                        [jnp.where(sub8 < 4,
                                   jnp.broadcast_to(ref_row(blk * 8 + 1), (8, LANES)),
                                   jnp.broadcast_to(ref_row(blk * 8 + 5), (8, LANES)))
                         for blk in range(CHUNK // 8)], axis=0)
                else:
                    cm = jnp.where((row_l & 1) == 1, pltpu.roll(cu, 1, axis=0), cu)
                right = ((row_l >> lm) & 1) == 1
                z = (jnp.where(right, qh, kh) * jnp.exp2(-jnp.abs(cu - cm))).astype(BF16)
                pair = ((xor_c >> lm) == 1) & (((row_c >> lm) & 1) == 1)
                att = jnp.where(pair, _nt(z, z), att)
            att = jnp.where(row_c == col_c, _nt(qh.astype(BF16), kh.astype(BF16)), att)
            head_update(h, base, att, qe_b, ke_b, d)
        return carry

    worst = front(0)
    for p in range(1, HGRN_PARTS):
        worst = jnp.maximum(worst, front(p))
    for c in range(n_chunks):
        fast_chunk(c * CHUNK)
    for p in range(HGRN_PARTS):
        back(p)

    @pl.when(jnp.max(worst) > MAX_LOCAL_EXPONENT * LOG2_E)
    def _():
        st_s[...] = st0_s[...]
        lax.fori_loop(0, n_chunks, slow_chunk, 0)
        for p in range(HGRN_PARTS):
            back(p)

    @pl.when(i == pl.num_programs(1) - 1)
    def _():
        for h in range(H_B):
            st_ref[h] = st_s[h].T


def _hgrn_prompt(x1, gain, w_in, lbl, o_gain, w_out, f_gain, *, tt):
    nb, total_t, _ = x1.shape
    return pl.pallas_call(
        functools.partial(_hgrn_prompt_kernel, tt),
        grid=(nb, total_t // tt),
        in_specs=[pl.BlockSpec((None, tt, D_MODEL), lambda b, i: (b, i, 0)),
                  _const_spec((1, D_MODEL)),
                  _const_spec((D_MODEL, 4 * INNER_B)),
                  _const_spec((2, INNER_B)),
                  _const_spec((1, INNER_B)),
                  _const_spec((INNER_B, D_MODEL)),
                  _const_spec((1, D_MODEL))],
        out_specs=[pl.BlockSpec((None, tt, D_MODEL), lambda b, i: (b, i, 0)),
                   pl.BlockSpec((None, H_B, DK_B, DV_B), lambda b, i: (b, 0, 0, 0))],
        out_shape=[jax.ShapeDtypeStruct((nb, total_t, D_MODEL), F32),
                   jax.ShapeDtypeStruct((nb, H_B, DK_B, DV_B), F32)],
        scratch_shapes=[pltpu.VMEM((tt, INNER_B), F32),
                        pltpu.VMEM((tt, INNER_B), F32),
                        pltpu.VMEM((tt, INNER_B), BF16),
                        pltpu.VMEM((tt, INNER_B), F32),
                        pltpu.VMEM((tt, INNER_B), F32),
                        pltpu.VMEM((tt, INNER_B), F32),
                        pltpu.VMEM((H_B, DV_B, DK_B), F32),
                        pltpu.VMEM((H_B, DV_B, DK_B), F32)],
        compiler_params=pltpu.CompilerParams(dimension_semantics=("arbitrary", "arbitrary"),
                                             vmem_limit_bytes=VMEM_LIMIT),
        name="hgrn_prompt",
    )(x1, gain, w_in, lbl, o_gain, w_out, f_gain)


def _hgrn_decode_kernel(nb, tt, bstep, x_ref, gain_ref, win_ref, lbl_ref, og_ref, wout_ref, fg_ref, s0_ref,
                        y_ref, snew_ref, qe_s, ke_s, dd_s, v_s, oi_s, gs_s, x_s, o_s):
    j = pl.program_id(0)
    rows = nb * tt

    def to_batch_major(dst, val, t):
        for h in range(H_B):
            dst[h, pl.ds(t, nb, stride=tt), :] = val[:, h * LANES:(h + 1) * LANES]

    @pl.when(j == 0)
    def _():
        x = x_ref[...]
        q, kk, v, lg, gs = _hgrn_front(x, gain_ref, win_ref, lbl_ref)
        ones_blk = jnp.ones((DK_B, DK_B), BF16)

        def slab(arr, t):
            return arr[t * nb:(t + 1) * nb, :]

        cum = [slab(lg, 0)]
        for t in range(1, tt):
            cum.append(cum[-1] + slab(lg, t))
        dd_s[...] = jnp.zeros_like(dd_s)
        d = jnp.exp(cum[tt - 1])
        d_hi = d.astype(BF16).astype(F32)
        to_batch_major(dd_s, d_hi, 0)
        to_batch_major(dd_s, d - d_hi, 1)
        for t in range(tt):
            qt = slab(q, t)
            acc = jnp.zeros((nb, INNER_B), F32)
            for s in range(t + 1):
                prod = qt * slab(kk, s)
                if s < t:
                    prod = prod * jnp.exp(cum[t] - cum[s])
                pb = prod.astype(BF16)
                att = jnp.concatenate(
                    [jnp.dot(pb[:, h * DK_B:(h + 1) * DK_B], ones_blk, preferred_element_type=F32)
                     for h in range(H_B)], axis=-1)
                acc = acc + att * slab(v, s)
            to_batch_major(oi_s, acc, t)
            to_batch_major(qe_s, qt * jnp.exp(cum[t]), t)
            to_batch_major(ke_s, slab(kk, t) * jnp.exp(cum[tt - 1] - cum[t]), t)
            to_batch_major(v_s, slab(v, t), t)
            to_batch_major(gs_s, slab(gs, t), t)
            to_batch_major(x_s, slab(x, t), t)

    row1 = lax.broadcasted_iota(jnp.int32, (8, LANES), 0)
    lane2 = lax.broadcasted_iota(jnp.int32, (8, 2 * DV_B), 1)
    row2 = lax.broadcasted_iota(jnp.int32, (8, 2 * DV_B), 0)
    own = (row1 < tt, row1 >= tt)
    ones_rows = tuple(
        jnp.where((lane2 >= DV_B) & (row2 >= lo) & (row2 < lo + 2), 1.0, 0.0).astype(BF16) for lo in (tt, 0))

    def pair_body(pi, carry):
        off = pl.multiple_of((j * (bstep // 2) + pi) * 8, 8)
        for h in range(H_B):
            qe = qe_s[h, pl.ds(off, 8), :].astype(BF16)
            ke = ke_s[h, pl.ds(off, 8), :]
            dd = pltpu.roll(dd_s[h, pl.ds(off, 8), :], tt, axis=0)
            vv = v_s[h, pl.ds(off, 8), :]
            inter = []
            for e in range(2):
                s0 = s0_ref[2 * pi + e, h]
                inter.append(jnp.dot(qe, s0.astype(BF16), preferred_element_type=F32))
                lhs = jnp.where(own[e], ke, dd).astype(BF16)
                ve = jnp.where(own[e], vv, 0.0).astype(BF16)
                rhs = jnp.concatenate([ve, jnp.zeros((8, DV_B), BF16)], axis=-1) + ones_rows[e]
                upd = lax.dot_general(lhs, rhs, (((0,), (0,)), ((), ())),
                                      preferred_element_type=F32)
                snew_ref[2 * pi + e, h] = upd[:, DV_B:] * s0 + upd[:, :DV_B]
            o_s[pl.ds(off, 8), h * DV_B:(h + 1) * DV_B] = (
                jnp.where(own[0], inter[0], inter[1]) + oi_s[h, pl.ds(off, 8), :])
        return carry

    lax.fori_loop(0, bstep // 2, pair_body, 0, unroll=True)

    @pl.when(j == pl.num_programs(0) - 1)
    def _():
        xb = jnp.concatenate([x_s[h] for h in range(H_B)], axis=-1)
        gsb = jnp.concatenate([gs_s[h] for h in range(H_B)], axis=-1)
        y_ref[...] = _hgrn_back(xb, o_s[...], gsb, og_ref, wout_ref, fg_ref)


def _hgrn_decode(x1, s0, gain, w_in, lbl, o_gain, w_out, f_gain, *, nb, tt, bstep):
    assert 2 * tt == 8 and bstep % 2 == 0
    rows = nb * tt
    slab = pltpu.VMEM((H_B, rows, LANES), F32)
    return pl.pallas_call(
        functools.partial(_hgrn_decode_kernel, nb, tt, bstep),
        grid=(nb // bstep,),
        in_specs=[_const_spec((rows, D_MODEL)),
                  _const_spec((1, D_MODEL)),
                  _const_spec((D_MODEL, 4 * INNER_B)),
                  _const_spec((2, INNER_B)),
                  _const_spec((1, INNER_B)),
                  _const_spec((INNER_B, D_MODEL)),
                  _const_spec((1, D_MODEL)),
                  pl.BlockSpec((bstep, H_B, DK_B, DV_B), lambda j: (j, 0, 0, 0))],
        out_specs=[pl.BlockSpec((rows, D_MODEL), lambda j: (0, 0)),
                   pl.BlockSpec((bstep, H_B, DK_B, DV_B), lambda j: (j, 0, 0, 0))],
        out_shape=[jax.ShapeDtypeStruct((rows, D_MODEL), F32),
                   jax.ShapeDtypeStruct((nb, H_B, DK_B, DV_B), F32)],
        scratch_shapes=[slab, slab, slab, slab, slab, slab, slab,
                        pltpu.VMEM((rows, INNER_B), F32)],
        compiler_params=pltpu.CompilerParams(dimension_semantics=("arbitrary",),
                                             vmem_limit_bytes=VMEM_LIMIT),
        name="hgrn_decode",
    )(x1, gain, w_in, lbl, o_gain, w_out, f_gain, s0)


def _gate_weights(w_r, w_i):
    per = GATE_SUPER // BLK_W

    def block_diag(w):
        w4 = w.reshape(N_SUPER, per, BLK_W, BLK_W)
        eye = jnp.eye(per, dtype=w.dtype)
        return jnp.einsum('saij,ac->saicj', w4, eye).reshape(N_SUPER, GATE_SUPER, GATE_SUPER)

    dr, di = block_diag(w_r), block_diag(w_i)
    out = []
    for j, (lo, hi) in enumerate(GATE_K_RANGES):
        cols = slice(j * LANES, (j + 1) * LANES)
        out.append(jnp.concatenate([dr[:, lo:hi, cols], di[:, lo:hi, cols]], axis=-1).astype(BF16))
    return out


def kernel(x_prompt, x_sample, state_lru_h, state_lru_conv, state_hgrn, norm_gain, a_w_in, a_conv_w, a_conv_b,
           a_w_r, a_b_r, a_w_i, a_b_i, a_lambda, a_w_out, b_w_in, b_lb_logits, b_o_gain, b_w_out, final_gain):
    assert norm_gain.shape[0] == 2 and a_w_in.shape[0] == 1 and b_w_in.shape[0] == 1
    pb, pt, _ = x_prompt.shape
    sb, st, _ = x_sample.shape
    halo_t = CONV_W - 1

    row = lambda p: p.reshape(1, -1)
    lru_col_scale = np.concatenate([np.ones((W_LRU,), np.float32), np.full((W_LRU,), 0.5, np.float32)])
    hgrn_col_scale = np.concatenate([np.full((2 * INNER_B,), 0.5, np.float32), np.ones((INNER_B,), np.float32),
                                     np.full((INNER_B,), 0.5, np.float32)])
    lru_params = (row(norm_gain[0]), (a_w_in[0] * lru_col_scale).astype(BF16), a_conv_w[0],
                  row(a_conv_b[0]), *_gate_weights(a_w_r[0], a_w_i[0]), row(a_b_r[0]),
                  row(a_b_i[0]), row(a_lambda[0]), a_w_out[0].astype(BF16))
    hgrn_params = (row(norm_gain[1]), (b_w_in[0] * hgrn_col_scale).astype(BF16), b_lb_logits, row(b_o_gain[0]),
                   b_w_out[0].astype(BF16), row(final_gain))

    x1p, hp, bufp = _lru_layer(x_prompt, jnp.zeros((pb, W_LRU), F32), jnp.zeros((halo_t * pb, W_LRU), F32),
                               *lru_params, nb=pb, tt=LRU_TILE_STEPS, batch_major=True)
    y_prompt, sp = _hgrn_prompt(x1p, *hgrn_params, tt=HGRN_TILE_ROWS)
    bufp = bufp.reshape(halo_t, pb, W_LRU).transpose(1, 0, 2)

    xs = x_sample.transpose(1, 0, 2).reshape(st * sb, D_MODEL)
    bufs0 = state_lru_conv[0].transpose(1, 0, 2).reshape(halo_t * sb, W_LRU)
    x1s, hs, bufs = _lru_layer(xs, state_lru_h[0], bufs0, *lru_params, nb=sb, tt=st, batch_major=False)
    ys, ss = _hgrn_decode(x1s, state_hgrn[0], *hgrn_params, nb=sb, tt=st, bstep=DECODE_SEQS_PER_STEP)
    bufs = bufs.reshape(halo_t, sb, W_LRU).transpose(1, 0, 2)

    return (y_prompt, ys.reshape(sb, st, D_MODEL), hp[None], bufp[None], sp[None],
            hs[None], bufs[None], ss[None])
```

```python
import functools

import numpy as np

import jax
import jax.numpy as jnp
from jax import lax
from jax.experimental import pallas as pl
from jax.experimental.pallas import tpu as pltpu

F32 = jnp.float32
BF16 = jnp.bfloat16

D_MODEL = 1024
W_LRU = 1536
N_BLK = 16
BLK_W = W_LRU // N_BLK
CONV_W = 4
LRU_C = 8.0
H_B = 8
DK_B = 128
DV_B = 128
INNER_B = H_B * DK_B
EPS = 1e-6

LANES = 128
GATE_SUPER = 384
N_SUPER = W_LRU // GATE_SUPER
LRU_PARTS = 1
HGRN_PARTS = 1
LRU_CHANNEL_BLOCK = 768
LRU_TILE_STEPS = 64
HGRN_COL_BLOCK = 512
HGRN_TILE_ROWS = 512
DECODE_SEQS_PER_STEP = 8
CHUNK = 64
LOG_CHUNK = 6
VMEM_LIMIT = 60 * 1024 * 1024
MAX_LOCAL_EXPONENT = 75.0
LOG2_E = 1.4426950408889634
LOG_FLOOR = -1e4


def _silu_from_half(hx):
    return hx + hx * jnp.tanh(hx)


def _rms_scale(x, gain):
    ms = jnp.mean(x * x, axis=-1, keepdims=True)
    return x * lax.rsqrt(ms + EPS) * gain


def _const_spec(shape):
    zeros = (0,) * len(shape)
    return pl.BlockSpec(shape, lambda *_: zeros, pipeline_mode=pl.Buffered(1))


def _gate_k_ranges():
    ranges = []
    for j in range(GATE_SUPER // LANES):
        first_blk = (j * LANES) // BLK_W
        last_blk = ((j + 1) * LANES - 1) // BLK_W
        lo = (first_blk * BLK_W) // LANES * LANES
        hi = -(-((last_blk + 1) * BLK_W) // LANES) * LANES
        ranges.append((lo, hi))
    return ranges


GATE_K_RANGES = _gate_k_ranges()


def _lru_kernel(nb, tt, batch_major, x_ref, gain_ref, win_ref, cw_ref, cb_ref, wg0_ref, wg1_ref, wg2_ref,
                br_ref, bi_ref, lam_ref, wout_ref, h0_ref, buf0_ref, x1_ref, hlast_ref, bufout_ref, *scratch):
    wg_refs = (wg0_ref, wg1_ref, wg2_ref)
    if batch_major:
        xs_s, os_s, xbuf_s, a_s, b_s, g_s, hc_s = scratch
    else:
        xbuf_s, a_s, b_s, g_s, hc_s = scratch
    rows = nb * tt
    halo = (CONV_W - 1) * nb
    n_lane_tiles = D_MODEL // LANES
    tp = tt // LRU_PARTS
    rp = nb * tp

    @pl.when(pl.program_id(0) == 0)
    def _():
        hc_s[...] = h0_ref[...]
        xbuf_s[0:halo, :] = buf0_ref[...]

    if batch_major:
        for b in range(nb):
            for j in range(n_lane_tiles):
                xs_s[j, pl.ds(b, tt, stride=nb), :] = x_ref[b, :, j * LANES:(j + 1) * LANES]

    def load_x(p):
        if batch_major:
            return jnp.concatenate([xs_s[j, p * rp:(p + 1) * rp, :] for j in range(n_lane_tiles)], axis=-1)
        return x_ref[p * rp:(p + 1) * rp, :]

    nlam = -lam_ref[...]
    softplus = jnp.maximum(nlam, 0.0) + jnp.log1p(jnp.exp(-jnp.abs(nlam)))
    half_rate = (-0.5 * LRU_C) * softplus
    half_cw = 0.5 * cw_ref[...]
    half_cb = 0.5 * cb_ref[...]
    half_br = 0.5 * br_ref[...]
    half_bi = 0.5 * bi_ref[...]

    def pre(p):
        r0 = p * rp
        xn = _rms_scale(load_x(p), gain_ref[...]).astype(BF16)
        for c0 in range(0, W_LRU, LRU_CHANNEL_BLOCK):
            cs = slice(c0, c0 + LRU_CHANNEL_BLOCK)
            xbuf_s[halo + r0:halo + r0 + rp, cs] = jnp.dot(xn, win_ref[:, cs], preferred_element_type=F32)
            xh = half_cb[:, cs] + half_cw[0:1, cs] * xbuf_s[r0:r0 + rp, cs]
            for k in range(1, CONV_W):
                xh = xh + half_cw[k:k + 1, cs] * xbuf_s[r0 + k * nb:r0 + k * nb + rp, cs]
            xhb = xh.astype(BF16)
            r_parts, i_parts = [], []
            for s in range(LRU_CHANNEL_BLOCK // GATE_SUPER):
                for (lo, hi), wg_ref in zip(GATE_K_RANGES, wg_refs):
                    gsi = jnp.dot(xhb[:, s * GATE_SUPER + lo:s * GATE_SUPER + hi],
                                  wg_ref[c0 // GATE_SUPER + s], preferred_element_type=F32)
                    r_parts.append(gsi[:, :LANES])
                    i_parts.append(gsi[:, LANES:])
            tr = jnp.tanh(jnp.concatenate(r_parts, axis=-1) + half_br[:, cs])
            ti = jnp.tanh(jnp.concatenate(i_parts, axis=-1) + half_bi[:, cs])
            log_a = half_rate[:, cs] + half_rate[:, cs] * tr
            a = jnp.exp(log_a)
            one_m_a2 = -jnp.tanh(log_a) * (a * a + 1.0)
            a_s[r0:r0 + rp, cs] = a
            root = jnp.where(one_m_a2 > 0.0, one_m_a2 * lax.rsqrt(one_m_a2), 0.0)
            b_s[r0:r0 + rp, cs] = root * (xh * (1.0 + ti))
            hg = jnp.dot(xn, win_ref[:, W_LRU + c0:W_LRU + c0 + LRU_CHANNEL_BLOCK], preferred_element_type=F32)
            g_s[r0:r0 + rp, cs] = _silu_from_half(hg)

    def scan(p, h):
        for t in range(p * tp, (p + 1) * tp):
            h = a_s[t * nb:(t + 1) * nb, :] * h + b_s[t * nb:(t + 1) * nb, :]
            b_s[t * nb:(t + 1) * nb, :] = h
        return h

    def post(p):
        r0 = p * rp
        y = (b_s[r0:r0 + rp, :] * g_s[r0:r0 + rp, :]).astype(BF16)
        out = load_x(p) + jnp.dot(y, wout_ref[...], preferred_element_type=F32)
        if batch_major:
            for j in range(n_lane_tiles):
                os_s[j, r0:r0 + rp, :] = out[:, j * LANES:(j + 1) * LANES]
            for b in range(nb):
                x1_ref[b, p * tp:(p + 1) * tp, :] = jnp.concatenate(
                    [os_s[j, pl.ds(r0 + b, tp, stride=nb), :] for j in range(n_lane_tiles)], axis=-1)
        else:
            x1_ref[r0:r0 + rp, :] = out

    for p in range(LRU_PARTS):
        pre(p)
    new_halo = xbuf_s[rows:rows + halo, :]
    bufout_ref[...] = new_halo
    xbuf_s[0:halo, :] = new_halo
    h = hc_s[...]
    for p in range(LRU_PARTS):
        h = scan(p, h)
        post(p)
    hc_s[...] = h
    hlast_ref[...] = h


def _lru_layer(x, h0, buf0, gain, w_in, conv_w, conv_b, wg0, wg1, wg2, b_r, b_i, lam, w_out, *, nb, tt, batch_major):
    if batch_major:
        total_t = x.shape[1]
        x_spec = pl.BlockSpec((nb, tt, D_MODEL), lambda i: (0, i, 0))
        x_shape = jax.ShapeDtypeStruct((nb, total_t, D_MODEL), F32)
    else:
        total_t = x.shape[0] // nb
        x_spec = pl.BlockSpec((nb * tt, D_MODEL), lambda i: (i, 0))
        x_shape = jax.ShapeDtypeStruct((nb * total_t, D_MODEL), F32)
    rows = nb * tt
    halo = (CONV_W - 1) * nb
    scratch = []
    if batch_major:
        scratch += [pltpu.VMEM((D_MODEL // LANES, rows, LANES), F32),
                    pltpu.VMEM((D_MODEL // LANES, rows, LANES), F32)]
    scratch += [pltpu.VMEM((rows + halo, W_LRU), F32),
                pltpu.VMEM((rows, W_LRU), F32),
                pltpu.VMEM((rows, W_LRU), F32),
                pltpu.VMEM((rows, W_LRU), F32),
                pltpu.VMEM((nb, W_LRU), F32)]
    return pl.pallas_call(
        functools.partial(_lru_kernel, nb, tt, batch_major),
        grid=(total_t // tt,),
        in_specs=[x_spec,
                  _const_spec((1, D_MODEL)),
                  _const_spec((D_MODEL, 2 * W_LRU)),
                  _const_spec((CONV_W, W_LRU)),
                  _const_spec((1, W_LRU)),
                  *[_const_spec((N_SUPER, hi - lo, 2 * LANES)) for lo, hi in GATE_K_RANGES],
                  _const_spec((1, W_LRU)),
                  _const_spec((1, W_LRU)),
                  _const_spec((1, W_LRU)),
                  _const_spec((W_LRU, D_MODEL)),
                  _const_spec((nb, W_LRU)),
                  _const_spec((halo, W_LRU))],
        out_specs=[x_spec,
                   pl.BlockSpec((nb, W_LRU), lambda i: (0, 0)),
                   pl.BlockSpec((halo, W_LRU), lambda i: (0, 0))],
        out_shape=[x_shape,
                   jax.ShapeDtypeStruct((nb, W_LRU), F32),
                   jax.ShapeDtypeStruct((halo, W_LRU), F32)],
        scratch_shapes=scratch,
        compiler_params=pltpu.CompilerParams(dimension_semantics=("arbitrary",),
                                             vmem_limit_bytes=VMEM_LIMIT),
        name="lru_layer_bm" if batch_major else "lru_layer_tm",
    )(x, gain, w_in, conv_w, conv_b, wg0, wg1, wg2, b_r, b_i, lam, w_out, h0, buf0)


def _lower_bound(lbl_ref):
    l0 = lbl_ref[0:1, :]
    l1 = lbl_ref[1:2, :]
    mx = jnp.maximum(l0, l1)
    e0 = jnp.exp(l0 - mx)
    e1 = jnp.exp(l1 - mx)
    return e1 / (e0 + e1)


def _hgrn_front(x, gain_ref, win_ref, lbl_ref):
    xn = _rms_scale(x, gain_ref[...]).astype(BF16)
    n = INNER_B
    q = _silu_from_half(jnp.dot(xn, win_ref[:, 0:n], preferred_element_type=F32))
    tf = jnp.tanh(jnp.dot(xn, win_ref[:, n:2 * n], preferred_element_type=F32))
    v = jnp.dot(xn, win_ref[:, 2 * n:3 * n], preferred_element_type=F32)
    gs = _silu_from_half(jnp.dot(xn, win_ref[:, 3 * n:4 * n], preferred_element_type=F32))
    lb = _lower_bound(lbl_ref)
    c1 = 0.5 * (1.0 - lb)
    ct = c1 * tf
    kk = c1 - ct
    g = (lb + c1) + ct
    lg = jnp.maximum(jnp.log(g), LOG_FLOOR)
    return q, kk, v, lg, gs


def _hgrn_back(x, o, gs, og_ref, wout_ref, fg_ref):
    parts = []
    for h in range(H_B):
        oh = o[:, h * DV_B:(h + 1) * DV_B]
        ms = jnp.mean(oh * oh, axis=-1, keepdims=True)
        parts.append(oh * lax.rsqrt(ms + EPS))
    on = jnp.concatenate(parts, axis=-1) * og_ref[...]
    y = (on * gs).astype(BF16)
    x2 = x + jnp.dot(y, wout_ref[...], preferred_element_type=F32)
    return _rms_scale(x2, fg_ref[...])


def _nt(a, b):
    return lax.dot_general(a, b, (((1,), (1,)), ((), ())), preferred_element_type=F32)


def _tn(a, b):
    return lax.dot_general(a, b, (((0,), (0,)), ((), ())), preferred_element_type=F32)


def _hgrn_prompt_kernel(tt, x_ref, gain_ref, win_ref, lbl_ref, og_ref, wout_ref, fg_ref,
                        y_ref, st_ref, q_s, k_s, v_s, cum_s, gs_s, o_s, st_s, st0_s):
    i = pl.program_id(1)
    n_chunks = tt // CHUNK
    half = CHUNK // 2
    rp = tt // HGRN_PARTS
    chunks_per_part = rp // CHUNK

    @pl.when(i == 0)
    def _():
        st_s[...] = jnp.zeros_like(st_s)

    st0_s[...] = st_s[...]

    row_c = lax.broadcasted_iota(jnp.int32, (CHUNK, CHUNK), 0)
    col_c = lax.broadcasted_iota(jnp.int32, (CHUNK, CHUNK), 1)
    tri = (row_c >= col_c).astype(BF16)
    tri3 = jnp.concatenate([tri, tri, tri], axis=-1)

    def front(p):
        r0 = p * rp
        n = INNER_B
        xn = _rms_scale(x_ref[r0:r0 + rp, :], gain_ref[...]).astype(BF16)
        lb = _lower_bound(lbl_ref)
        c1 = 0.5 * (1.0 - lb)
        worst_blocks = []
        for c0 in range(0, n, HGRN_COL_BLOCK):
            cs = slice(c0, c0 + HGRN_COL_BLOCK)

            def proj(branch):
                lo = branch * n + c0
                return jnp.dot(xn, win_ref[:, lo:lo + HGRN_COL_BLOCK], preferred_element_type=F32)

            q_s[r0:r0 + rp, cs] = _silu_from_half(proj(0))
            ct = c1[:, cs] * jnp.tanh(proj(1))
            k_s[r0:r0 + rp, cs] = c1[:, cs] - ct
            g = (lb[:, cs] + c1[:, cs]) + ct
            v_s[r0:r0 + rp, cs] = proj(2).astype(BF16)
            gs_s[r0:r0 + rp, cs] = _silu_from_half(proj(3))
            lg = jnp.maximum(jnp.log(g), LOG_FLOOR) * LOG2_E
            h1 = lg.astype(BF16)
            r1 = lg - h1.astype(F32)
            h2 = r1.astype(BF16)
            h3 = (r1 - h2.astype(F32)).astype(BF16)
            worst = jnp.zeros((1, HGRN_COL_BLOCK), F32)
            for c in range(chunks_per_part):
                rs = slice(c * CHUNK, (c + 1) * CHUNK)
                cum = jnp.dot(tri3, jnp.concatenate([h1[rs], h2[rs], h3[rs]], axis=0), preferred_element_type=F32)
                cum_s[r0 + c * CHUNK:r0 + (c + 1) * CHUNK, cs] = cum
                mid = cum[half - 1:half, :]
                worst = jnp.maximum(worst, jnp.maximum(-mid, mid - cum[CHUNK - 1:CHUNK, :]))
            worst_blocks.append(worst)
        return jnp.concatenate(worst_blocks, axis=-1)

    def back(p):
        r0 = p * rp
        y_ref[r0:r0 + rp, :] = _hgrn_back(x_ref[r0:r0 + rp, :], o_s[r0:r0 + rp, :], gs_s[r0:r0 + rp, :],
                                           og_ref, wout_ref, fg_ref)

    def head_update(h, base, att, qe_b, ke_b, d):
        sl = slice(h * DK_B, (h + 1) * DK_B)
        vh = v_s[pl.ds(base, CHUNK), sl]
        st = st_s[h]
        o = _nt(qe_b[:, sl], st.astype(BF16)) + jnp.dot(att.astype(BF16), vh, preferred_element_type=F32)
        st_s[h] = d[:, sl] * st + _tn(vh, ke_b[:, sl])
        o_s[pl.ds(base, CHUNK), sl] = o

    def chunk_common(base):
        cu = cum_s[pl.ds(base, CHUNK), :]
        tail = cum_s[pl.ds(base + CHUNK - 8, 8), :]
        clast = tail[7:8, :]
        qf = q_s[pl.ds(base, CHUNK), :]
        kf = k_s[pl.ds(base, CHUNK), :]
        qe = qf * jnp.exp2(cu)
        ke_b = (kf * jnp.exp2(clast - cu)).astype(BF16)
        return cu, qf, kf, qe, ke_b, jnp.exp2(clast)

    def fast_chunk(base):
        cu_l = cum_s[pl.ds(base, half), :]
        cu_r = cum_s[pl.ds(base + half, half), :]
        q_l, q_r = q_s[pl.ds(base, half), :], q_s[pl.ds(base + half, half), :]
        k_l, k_r = k_s[pl.ds(base, half), :], k_s[pl.ds(base + half, half), :]
        cmid = cu_l[half - 1:half, :]
        clast = cu_r[half - 1:half, :]
        bf = lambda z: z.astype(BF16)
        qe_l, qe_r = bf(q_l * jnp.exp2(cu_l)), bf(q_r * jnp.exp2(cu_r))
        ke_l, ke_r = bf(k_l * jnp.exp2(clast - cu_l)), bf(k_r * jnp.exp2(clast - cu_r))
        za_l = bf(k_l * jnp.exp2(cmid - cu_l))
        za_r = bf(q_r * jnp.exp2(cu_r - cmid))
        kl_l = bf(k_l * jnp.exp2(-cu_l))
        kl_r = bf(k_r * jnp.exp2(cmid - cu_r))
        qe_b = jnp.concatenate([qe_l, qe_r], axis=0)
        ke_b = jnp.concatenate([ke_l, ke_r], axis=0)
        za_b = jnp.concatenate([za_l, za_r], axis=0)
        kl_b = jnp.concatenate([kl_l, kl_r], axis=0)
        ql_b = jnp.concatenate([qe_l, za_r], axis=0)
        d = jnp.exp2(clast)
        cross = (row_c >= half) & (col_c < half)
        local = (row_c >= col_c) & ((row_c >= half) == (col_c >= half))
        for h in range(H_B):
            sl = slice(h * DK_B, (h + 1) * DK_B)
            pa = _nt(za_b[:, sl], za_b[:, sl])
            pb = _nt(ql_b[:, sl], kl_b[:, sl])
            att = jnp.where(cross, pa, jnp.where(local, pb, 0.0))
            head_update(h, base, att, qe_b, ke_b, d)

    def slow_chunk(c, carry):
        base = pl.multiple_of(c * CHUNK, CHUNK)
        cu_all, qf, kf, qe, ke_b, d = chunk_common(base)
        qe_b = qe.astype(BF16)
        xor_c = row_c ^ col_c
        row_l = lax.broadcasted_iota(jnp.int32, (CHUNK, LANES), 0)
        sub8 = lax.broadcasted_iota(jnp.int32, (8, LANES), 0)
        for h in range(H_B):
            sl = slice(h * DK_B, (h + 1) * DK_B)
            cu = cu_all[:, sl]
            qh = qf[:, sl]
            kh = kf[:, sl]

            def ref_row(r):
                grp = cu[(r // 8) * 8:(r // 8) * 8 + 8, :]
                return grp[r % 8:r % 8 + 1, :]

            att = jnp.zeros((CHUNK, CHUNK), F32)
            for lm in range(LOG_CHUNK):
                m = 1 << lm
                if lm >= 2:
                    cm = jnp.concatenate(
                        [jnp.broadcast_to(ref_row(blk * 2 * m + m - 1), (2 * m, LANES))
                         for blk in range(CHUNK // (2 * m))], axis=0)
                elif lm == 1:
                    cm = jnp.concatenate(
                        [jnp.where(sub8 < 4,
                                   jnp.broadcast_to(ref_row(blk * 8 + 1), (8, LANES)),
                                   jnp.broadcast_to(ref_row(blk * 8 + 5), (8, LANES)))
                         for blk in range(CHUNK // 8)], axis=0)
                else:
                    cm = jnp.where((row_l & 1) == 1, pltpu.roll(cu, 1, axis=0), cu)
                right = ((row_l >> lm) & 1) == 1
                z = (jnp.where(right, qh, kh) * jnp.exp2(-jnp.abs(cu - cm))).astype(BF16)
                pair = ((xor_c >> lm) == 1) & (((row_c >> lm) & 1) == 1)
                att = jnp.where(pair, _nt(z, z), att)
            att = jnp.where(row_c == col_c, _nt(qh.astype(BF16), kh.astype(BF16)), att)
            head_update(h, base, att, qe_b, ke_b, d)
        return carry

    worst = front(0)
    for p in range(1, HGRN_PARTS):
        worst = jnp.maximum(worst, front(p))
    for c in range(n_chunks):
        fast_chunk(c * CHUNK)
    for p in range(HGRN_PARTS):
        back(p)

    @pl.when(jnp.max(worst) > MAX_LOCAL_EXPONENT * LOG2_E)
    def _():
        st_s[...] = st0_s[...]
        lax.fori_loop(0, n_chunks, slow_chunk, 0)
        for p in range(HGRN_PARTS):
            back(p)

    @pl.when(i == pl.num_programs(1) - 1)
    def _():
        for h in range(H_B):
            st_ref[h] = st_s[h].T


def _hgrn_prompt(x1, gain, w_in, lbl, o_gain, w_out, f_gain, *, tt):
    nb, total_t, _ = x1.shape
    return pl.pallas_call(
        functools.partial(_hgrn_prompt_kernel, tt),
        grid=(nb, total_t // tt),
        in_specs=[pl.BlockSpec((None, tt, D_MODEL), lambda b, i: (b, i, 0)),
                  _const_spec((1, D_MODEL)),
                  _const_spec((D_MODEL, 4 * INNER_B)),
                  _const_spec((2, INNER_B)),
                  _const_spec((1, INNER_B)),
                  _const_spec((INNER_B, D_MODEL)),
                  _const_spec((1, D_MODEL))],
        out_specs=[pl.BlockSpec((None, tt, D_MODEL), lambda b, i: (b, i, 0)),
                   pl.BlockSpec((None, H_B, DK_B, DV_B), lambda b, i: (b, 0, 0, 0))],
        out_shape=[jax.ShapeDtypeStruct((nb, total_t, D_MODEL), F32),
                   jax.ShapeDtypeStruct((nb, H_B, DK_B, DV_B), F32)],
        scratch_shapes=[pltpu.VMEM((tt, INNER_B), F32),
                        pltpu.VMEM((tt, INNER_B), F32),
                        pltpu.VMEM((tt, INNER_B), BF16),
                        pltpu.VMEM((tt, INNER_B), F32),
                        pltpu.VMEM((tt, INNER_B), F32),
                        pltpu.VMEM((tt, INNER_B), F32),
                        pltpu.VMEM((H_B, DV_B, DK_B), F32),
                        pltpu.VMEM((H_B, DV_B, DK_B), F32)],
        compiler_params=pltpu.CompilerParams(dimension_semantics=("arbitrary", "arbitrary"),
                                             vmem_limit_bytes=VMEM_LIMIT),
        name="hgrn_prompt",
    )(x1, gain, w_in, lbl, o_gain, w_out, f_gain)


def _hgrn_decode_kernel(nb, tt, bstep, x_ref, gain_ref, win_ref, lbl_ref, og_ref, wout_ref, fg_ref, s0_ref,
                        y_ref, snew_ref, qe_s, ke_s, dd_s, v_s, oi_s, gs_s, x_s, o_s):
    j = pl.program_id(0)
    rows = nb * tt

    def to_batch_major(dst, val, t):
        for h in range(H_B):
            dst[h, pl.ds(t, nb, stride=tt), :] = val[:, h * LANES:(h + 1) * LANES]

    @pl.when(j == 0)
    def _():
        x = x_ref[...]
        q, kk, v, lg, gs = _hgrn_front(x, gain_ref, win_ref, lbl_ref)
        ones_blk = jnp.ones((DK_B, DK_B), BF16)

        def slab(arr, t):
            return arr[t * nb:(t + 1) * nb, :]

        cum = [slab(lg, 0)]
        for t in range(1, tt):
            cum.append(cum[-1] + slab(lg, t))
        dd_s[...] = jnp.zeros_like(dd_s)
        d = jnp.exp(cum[tt - 1])
        d_hi = d.astype(BF16).astype(F32)
        to_batch_major(dd_s, d_hi, 0)
        to_batch_major(dd_s, d - d_hi, 1)
        for t in range(tt):
            qt = slab(q, t)
            acc = jnp.zeros((nb, INNER_B), F32)
            for s in range(t + 1):
                prod = qt * slab(kk, s)
                if s < t:
                    prod = prod * jnp.exp(cum[t] - cum[s])
                pb = prod.astype(BF16)
                att = jnp.concatenate(
                    [jnp.dot(pb[:, h * DK_B:(h + 1) * DK_B], ones_blk, preferred_element_type=F32)
                     for h in range(H_B)], axis=-1)
                acc = acc + att * slab(v, s)
            to_batch_major(oi_s, acc, t)
            to_batch_major(qe_s, qt * jnp.exp(cum[t]), t)
            to_batch_major(ke_s, slab(kk, t) * jnp.exp(cum[tt - 1] - cum[t]), t)
            to_batch_major(v_s, slab(v, t), t)
            to_batch_major(gs_s, slab(gs, t), t)
            to_batch_major(x_s, slab(x, t), t)

    row1 = lax.broadcasted_iota(jnp.int32, (8, LANES), 0)
    lane2 = lax.broadcasted_iota(jnp.int32, (8, 2 * DV_B), 1)
    row2 = lax.broadcasted_iota(jnp.int32, (8, 2 * DV_B), 0)
    own = (row1 < tt, row1 >= tt)
    ones_rows = tuple(
        jnp.where((lane2 >= DV_B) & (row2 >= lo) & (row2 < lo + 2), 1.0, 0.0).astype(BF16) for lo in (tt, 0))

    def pair_body(pi, carry):
        off = pl.multiple_of((j * (bstep // 2) + pi) * 8, 8)
        for h in range(H_B):
            qe = qe_s[h, pl.ds(off, 8), :].astype(BF16)
            ke = ke_s[h, pl.ds(off, 8), :]
            dd = pltpu.roll(dd_s[h, pl.ds(off, 8), :], tt, axis=0)
            vv = v_s[h, pl.ds(off, 8), :]
            inter = []
            for e in range(2):
                s0 = s0_ref[2 * pi + e, h]
                inter.append(jnp.dot(qe, s0.astype(BF16), preferred_element_type=F32))
                lhs = jnp.where(own[e], ke, dd).astype(BF16)
                ve = jnp.where(own[e], vv, 0.0).astype(BF16)
                rhs = jnp.concatenate([ve, jnp.zeros((8, DV_B), BF16)], axis=-1) + ones_rows[e]
                upd = lax.dot_general(lhs, rhs, (((0,), (0,)), ((), ())),
                                      preferred_element_type=F32)
                snew_ref[2 * pi + e, h] = upd[:, DV_B:] * s0 + upd[:, :DV_B]
            o_s[pl.ds(off, 8), h * DV_B:(h + 1) * DV_B] = (
                jnp.where(own[0], inter[0], inter[1]) + oi_s[h, pl.ds(off, 8), :])
        return carry

    lax.fori_loop(0, bstep // 2, pair_body, 0, unroll=True)

    @pl.when(j == pl.num_programs(0) - 1)
    def _():
        xb = jnp.concatenate([x_s[h] for h in range(H_B)], axis=-1)
        gsb = jnp.concatenate([gs_s[h] for h in range(H_B)], axis=-1)
        y_ref[...] = _hgrn_back(xb, o_s[...], gsb, og_ref, wout_ref, fg_ref)


def _hgrn_decode(x1, s0, gain, w_in, lbl, o_gain, w_out, f_gain, *, nb, tt, bstep):
    assert 2 * tt == 8 and bstep % 2 == 0
    rows = nb * tt
    slab = pltpu.VMEM((H_B, rows, LANES), F32)
    return pl.pallas_call(
        functools.partial(_hgrn_decode_kernel, nb, tt, bstep),
        grid=(nb // bstep,),
        in_specs=[_const_spec((rows, D_MODEL)),
                  _const_spec((1, D_MODEL)),
                  _const_spec((D_MODEL, 4 * INNER_B)),
                  _const_spec((2, INNER_B)),
                  _const_spec((1, INNER_B)),
                  _const_spec((INNER_B, D_MODEL)),
                  _const_spec((1, D_MODEL)),
                  pl.BlockSpec((bstep, H_B, DK_B, DV_B), lambda j: (j, 0, 0, 0))],
        out_specs=[pl.BlockSpec((rows, D_MODEL), lambda j: (0, 0)),
                   pl.BlockSpec((bstep, H_B, DK_B, DV_B), lambda j: (j, 0, 0, 0))],
        out_shape=[jax.ShapeDtypeStruct((rows, D_MODEL), F32),
                   jax.ShapeDtypeStruct((nb, H_B, DK_B, DV_B), F32)],
        scratch_shapes=[slab, slab, slab, slab, slab, slab, slab,
                        pltpu.VMEM((rows, INNER_B), F32)],
        compiler_params=pltpu.CompilerParams(dimension_semantics=("arbitrary",),
                                             vmem_limit_bytes=VMEM_LIMIT),
        name="hgrn_decode",
    )(x1, gain, w_in, lbl, o_gain, w_out, f_gain, s0)


def _gate_weights(w_r, w_i):
    per = GATE_SUPER // BLK_W

    def block_diag(w):
        w4 = w.reshape(N_SUPER, per, BLK_W, BLK_W)
        eye = jnp.eye(per, dtype=w.dtype)
        return jnp.einsum('saij,ac->saicj', w4, eye).reshape(N_SUPER, GATE_SUPER, GATE_SUPER)

    dr, di = block_diag(w_r), block_diag(w_i)
    out = []
    for j, (lo, hi) in enumerate(GATE_K_RANGES):
        cols = slice(j * LANES, (j + 1) * LANES)
        out.append(jnp.concatenate([dr[:, lo:hi, cols], di[:, lo:hi, cols]], axis=-1).astype(BF16))
    return out


def kernel(x_prompt, x_sample, state_lru_h, state_lru_conv, state_hgrn, norm_gain, a_w_in, a_conv_w, a_conv_b,
           a_w_r, a_b_r, a_w_i, a_b_i, a_lambda, a_w_out, b_w_in, b_lb_logits, b_o_gain, b_w_out, final_gain):
    assert norm_gain.shape[0] == 2 and a_w_in.shape[0] == 1 and b_w_in.shape[0] == 1
    pb, pt, _ = x_prompt.shape
    sb, st, _ = x_sample.shape
    halo_t = CONV_W - 1

    row = lambda p: p.reshape(1, -1)
    lru_col_scale = np.concatenate([np.ones((W_LRU,), np.float32), np.full((W_LRU,), 0.5, np.float32)])
    hgrn_col_scale = np.concatenate([np.full((2 * INNER_B,), 0.5, np.float32), np.ones((INNER_B,), np.float32),
                                     np.full((INNER_B,), 0.5, np.float32)])
    lru_params = (row(norm_gain[0]), (a_w_in[0] * lru_col_scale).astype(BF16), a_conv_w[0],
                  row(a_conv_b[0]), *_gate_weights(a_w_r[0], a_w_i[0]), row(a_b_r[0]),
                  row(a_b_i[0]), row(a_lambda[0]), a_w_out[0].astype(BF16))
    hgrn_params = (row(norm_gain[1]), (b_w_in[0] * hgrn_col_scale).astype(BF16), b_lb_logits, row(b_o_gain[0]),
                   b_w_out[0].astype(BF16), row(final_gain))

    x1p, hp, bufp = _lru_layer(x_prompt, jnp.zeros((pb, W_LRU), F32), jnp.zeros((halo_t * pb, W_LRU), F32),
                               *lru_params, nb=pb, tt=LRU_TILE_STEPS, batch_major=True)
    y_prompt, sp = _hgrn_prompt(x1p, *hgrn_params, tt=HGRN_TILE_ROWS)
    bufp = bufp.reshape(halo_t, pb, W_LRU).transpose(1, 0, 2)

    xs = x_sample.transpose(1, 0, 2).reshape(st * sb, D_MODEL)
    bufs0 = state_lru_conv[0].transpose(1, 0, 2).reshape(halo_t * sb, W_LRU)
    x1s, hs, bufs = _lru_layer(xs, state_lru_h[0], bufs0, *lru_params, nb=sb, tt=st, batch_major=False)
    ys, ss = _hgrn_decode(x1s, state_hgrn[0], *hgrn_params, nb=sb, tt=st, bstep=DECODE_SEQS_PER_STEP)
    bufs = bufs.reshape(halo_t, sb, W_LRU).transpose(1, 0, 2)

    return (y_prompt, ys.reshape(sb, st, D_MODEL), hp[None], bufp[None], sp[None],
            hs[None], bufs[None], ss[None])
```

```python
import functools

import numpy as np

import jax
import jax.numpy as jnp
from jax import lax
from jax.experimental import pallas as pl
from jax.experimental.pallas import tpu as pltpu

F32 = jnp.float32
BF16 = jnp.bfloat16

D_MODEL = 1024
W_LRU = 1536
N_BLK = 16
BLK_W = W_LRU // N_BLK
CONV_W = 4
LRU_C = 8.0
H_B = 8
DK_B = 128
DV_B = 128
INNER_B = H_B * DK_B
EPS = 1e-6

LANES = 128
GATE_SUPER = 384
N_SUPER = W_LRU // GATE_SUPER
LRU_PARTS = 1
HGRN_PARTS = 1
LRU_CHANNEL_BLOCK = 768
LRU_TILE_STEPS = 64
BACK_HEADS = 2
HGRN_COL_BLOCK = 512
HGRN_TILE_ROWS = 512
DECODE_SEQS_PER_STEP = 8
CHUNK = 64
LOG_CHUNK = 6
VMEM_LIMIT = 60 * 1024 * 1024
MAX_LOCAL_EXPONENT = 75.0
LOG2_E = 1.4426950408889634
LOG_FLOOR = -1e4


def _silu_from_half(hx):
    return hx + hx * jnp.tanh(hx)


def _rms_scale(x, gain):
    ms = jnp.mean(x * x, axis=-1, keepdims=True)
    return x * lax.rsqrt(ms + EPS) * gain


def _const_spec(shape):
    zeros = (0,) * len(shape)
    return pl.BlockSpec(shape, lambda *_: zeros, pipeline_mode=pl.Buffered(1))


def _gate_k_ranges():
    ranges = []
    for j in range(GATE_SUPER // LANES):
        first_blk = (j * LANES) // BLK_W
        last_blk = ((j + 1) * LANES - 1) // BLK_W
        lo = (first_blk * BLK_W) // LANES * LANES
        hi = -(-((last_blk + 1) * BLK_W) // LANES) * LANES
        ranges.append((lo, hi))
    return ranges


GATE_K_RANGES = _gate_k_ranges()


def _lru_kernel(nb, tt, batch_major, x_ref, gain_ref, win_ref, cw_ref, cb_ref, wg0_ref, wg1_ref, wg2_ref,
                br_ref, bi_ref, lam_ref, wout_ref, h0_ref, buf0_ref, x1_ref, hlast_ref, bufout_ref, *scratch):
    wg_refs = (wg0_ref, wg1_ref, wg2_ref)
    if batch_major:
        xs_s, os_s, xbuf_s, a_s, b_s, g_s, hc_s = scratch
    else:
        xbuf_s, a_s, b_s, g_s, hc_s = scratch
    rows = nb * tt
    halo = (CONV_W - 1) * nb
    n_lane_tiles = D_MODEL // LANES
    tp = tt // LRU_PARTS
    rp = nb * tp

    @pl.when(pl.program_id(0) == 0)
    def _():
        hc_s[...] = h0_ref[...]
        xbuf_s[0:halo, :] = buf0_ref[...]

    if batch_major:
        for b in range(nb):
            for j in range(n_lane_tiles):
                xs_s[j, pl.ds(b, tt, stride=nb), :] = x_ref[b, :, j * LANES:(j + 1) * LANES]

    def load_x(p):
        if batch_major:
            return jnp.concatenate([xs_s[j, p * rp:(p + 1) * rp, :] for j in range(n_lane_tiles)], axis=-1)
        return x_ref[p * rp:(p + 1) * rp, :]

    nlam = -lam_ref[...]
    softplus = jnp.maximum(nlam, 0.0) + jnp.log1p(jnp.exp(-jnp.abs(nlam)))
    half_rate = (-0.5 * LRU_C) * softplus
    half_cw = 0.5 * cw_ref[...]
    half_cb = 0.5 * cb_ref[...]
    half_br = 0.5 * br_ref[...]
    half_bi = 0.5 * bi_ref[...]

    def pre(p):
        r0 = p * rp
        xn = _rms_scale(load_x(p), gain_ref[...]).astype(BF16)
        for c0 in range(0, W_LRU, LRU_CHANNEL_BLOCK):
            cs = slice(c0, c0 + LRU_CHANNEL_BLOCK)
            xbuf_s[halo + r0:halo + r0 + rp, cs] = jnp.dot(xn, win_ref[:, cs], preferred_element_type=F32)
            xh = half_cb[:, cs] + half_cw[0:1, cs] * xbuf_s[r0:r0 + rp, cs]
            for k in range(1, CONV_W):
                xh = xh + half_cw[k:k + 1, cs] * xbuf_s[r0 + k * nb:r0 + k * nb + rp, cs]
            xhb = xh.astype(BF16)
            r_parts, i_parts = [], []
            for s in range(LRU_CHANNEL_BLOCK // GATE_SUPER):
                for (lo, hi), wg_ref in zip(GATE_K_RANGES, wg_refs):
                    gsi = jnp.dot(xhb[:, s * GATE_SUPER + lo:s * GATE_SUPER + hi],
                                  wg_ref[c0 // GATE_SUPER + s], preferred_element_type=F32)
                    r_parts.append(gsi[:, :LANES])
                    i_parts.append(gsi[:, LANES:])
            tr = jnp.tanh(jnp.concatenate(r_parts, axis=-1) + half_br[:, cs])
            ti = jnp.tanh(jnp.concatenate(i_parts, axis=-1) + half_bi[:, cs])
            log_a = half_rate[:, cs] + half_rate[:, cs] * tr
            a = jnp.exp(log_a)
            one_m_a2 = -jnp.tanh(log_a) * (a * a + 1.0)
            a_s[r0:r0 + rp, cs] = a
            root = jnp.where(one_m_a2 > 0.0, one_m_a2 * lax.rsqrt(one_m_a2), 0.0)
            b_s[r0:r0 + rp, cs] = root * (xh * (1.0 + ti))
            hg = jnp.dot(xn, win_ref[:, W_LRU + c0:W_LRU + c0 + LRU_CHANNEL_BLOCK], preferred_element_type=F32)
            g_s[r0:r0 + rp, cs] = _silu_from_half(hg)

    def scan(p, h):
        for t in range(p * tp, (p + 1) * tp):
            h = a_s[t * nb:(t + 1) * nb, :] * h + b_s[t * nb:(t + 1) * nb, :]
            b_s[t * nb:(t + 1) * nb, :] = h
        return h

    def post(p):
        r0 = p * rp
        y = (b_s[r0:r0 + rp, :] * g_s[r0:r0 + rp, :]).astype(BF16)
        out = load_x(p) + jnp.dot(y, wout_ref[...], preferred_element_type=F32)
        if batch_major:
            for j in range(n_lane_tiles):
                os_s[j, r0:r0 + rp, :] = out[:, j * LANES:(j + 1) * LANES]
            for b in range(nb):
                x1_ref[b, p * tp:(p + 1) * tp, :] = jnp.concatenate(
                    [os_s[j, pl.ds(r0 + b, tp, stride=nb), :] for j in range(n_lane_tiles)], axis=-1)
        else:
            x1_ref[r0:r0 + rp, :] = out

    for p in range(LRU_PARTS):
        pre(p)
    new_halo = xbuf_s[rows:rows + halo, :]
    bufout_ref[...] = new_halo
    xbuf_s[0:halo, :] = new_halo
    h = hc_s[...]
    for p in range(LRU_PARTS):
        h = scan(p, h)
        post(p)
    hc_s[...] = h
    hlast_ref[...] = h


def _lru_layer(x, h0, buf0, gain, w_in, conv_w, conv_b, wg0, wg1, wg2, b_r, b_i, lam, w_out, *, nb, tt, batch_major):
    if batch_major:
        total_t = x.shape[1]
        x_spec = pl.BlockSpec((nb, tt, D_MODEL), lambda i: (0, i, 0))
        x_shape = jax.ShapeDtypeStruct((nb, total_t, D_MODEL), F32)
    else:
        total_t = x.shape[0] // nb
        x_spec = pl.BlockSpec((nb * tt, D_MODEL), lambda i: (i, 0))
        x_shape = jax.ShapeDtypeStruct((nb * total_t, D_MODEL), F32)
    rows = nb * tt
    halo = (CONV_W - 1) * nb
    scratch = []
    if batch_major:
        scratch += [pltpu.VMEM((D_MODEL // LANES, rows, LANES), F32),
                    pltpu.VMEM((D_MODEL // LANES, rows, LANES), F32)]
    scratch += [pltpu.VMEM((rows + halo, W_LRU), F32),
                pltpu.VMEM((rows, W_LRU), F32),
                pltpu.VMEM((rows, W_LRU), F32),
                pltpu.VMEM((rows, W_LRU), F32),
                pltpu.VMEM((nb, W_LRU), F32)]
    return pl.pallas_call(
        functools.partial(_lru_kernel, nb, tt, batch_major),
        grid=(total_t // tt,),
        in_specs=[x_spec,
                  _const_spec((1, D_MODEL)),
                  _const_spec((D_MODEL, 2 * W_LRU)),
                  _const_spec((CONV_W, W_LRU)),
                  _const_spec((1, W_LRU)),
                  *[_const_spec((N_SUPER, hi - lo, 2 * LANES)) for lo, hi in GATE_K_RANGES],
                  _const_spec((1, W_LRU)),
                  _const_spec((1, W_LRU)),
                  _const_spec((1, W_LRU)),
                  _const_spec((W_LRU, D_MODEL)),
                  _const_spec((nb, W_LRU)),
                  _const_spec((halo, W_LRU))],
        out_specs=[x_spec,
                   pl.BlockSpec((nb, W_LRU), lambda i: (0, 0)),
                   pl.BlockSpec((halo, W_LRU), lambda i: (0, 0))],
        out_shape=[x_shape,
                   jax.ShapeDtypeStruct((nb, W_LRU), F32),
                   jax.ShapeDtypeStruct((halo, W_LRU), F32)],
        scratch_shapes=scratch,
        compiler_params=pltpu.CompilerParams(dimension_semantics=("arbitrary",),
                                             vmem_limit_bytes=VMEM_LIMIT),
        name="lru_layer_bm" if batch_major else "lru_layer_tm",
    )(x, gain, w_in, conv_w, conv_b, wg0, wg1, wg2, b_r, b_i, lam, w_out, h0, buf0)


def _lower_bound(lbl_ref):
    l0 = lbl_ref[0:1, :]
    l1 = lbl_ref[1:2, :]
    mx = jnp.maximum(l0, l1)
    e0 = jnp.exp(l0 - mx)
    e1 = jnp.exp(l1 - mx)
    return e1 / (e0 + e1)


def _hgrn_front(x, gain_ref, win_ref, lbl_ref):
    xn = _rms_scale(x, gain_ref[...]).astype(BF16)
    n = INNER_B
    q = _silu_from_half(jnp.dot(xn, win_ref[:, 0:n], preferred_element_type=F32))
    tf = jnp.tanh(jnp.dot(xn, win_ref[:, n:2 * n], preferred_element_type=F32))
    v = jnp.dot(xn, win_ref[:, 2 * n:3 * n], preferred_element_type=F32)
    gs = _silu_from_half(jnp.dot(xn, win_ref[:, 3 * n:4 * n], preferred_element_type=F32))
    lb = _lower_bound(lbl_ref)
    c1 = 0.5 * (1.0 - lb)
    ct = c1 * tf
    kk = c1 - ct
    g = (lb + c1) + ct
    lg = jnp.maximum(jnp.log(g), LOG_FLOOR)
    return q, kk, v, lg, gs


def _hgrn_back(x, o, gs, og_ref, wout_ref, fg_ref):
    x2 = x
    for h0 in range(0, H_B, BACK_HEADS):
        parts = []
        for h in range(h0, h0 + BACK_HEADS):
            oh = o[:, h * DV_B:(h + 1) * DV_B]
            ms = jnp.mean(oh * oh, axis=-1, keepdims=True)
            parts.append(oh * lax.rsqrt(ms + EPS))
        cs = slice(h0 * DV_B, (h0 + BACK_HEADS) * DV_B)
        y = (jnp.concatenate(parts, axis=-1) * og_ref[:, cs] * gs[:, cs]).astype(BF16)
        x2 = x2 + jnp.dot(y, wout_ref[cs, :], preferred_element_type=F32)
    return _rms_scale(x2, fg_ref[...])


def _nt(a, b):
    return lax.dot_general(a, b, (((1,), (1,)), ((), ())), preferred_element_type=F32)


def _tn(a, b):
    return lax.dot_general(a, b, (((0,), (0,)), ((), ())), preferred_element_type=F32)


def _hgrn_prompt_kernel(tt, x_ref, gain_ref, win_ref, lbl_ref, og_ref, wout_ref, fg_ref,
                        y_ref, st_ref, q_s, k_s, v_s, cum_s, gs_s, o_s, st_s, st0_s):
    i = pl.program_id(1)
    n_chunks = tt // CHUNK
    half = CHUNK // 2
    rp = tt // HGRN_PARTS
    chunks_per_part = rp // CHUNK

    @pl.when(i == 0)
    def _():
        st_s[...] = jnp.zeros_like(st_s)

    st0_s[...] = st_s[...]

    row_c = lax.broadcasted_iota(jnp.int32, (CHUNK, CHUNK), 0)
    col_c = lax.broadcasted_iota(jnp.int32, (CHUNK, CHUNK), 1)
    tri = (row_c >= col_c).astype(BF16)
    tri3 = jnp.concatenate([tri, tri, tri], axis=-1)

    def front(p):
        r0 = p * rp
        n = INNER_B
        xn = _rms_scale(x_ref[r0:r0 + rp, :], gain_ref[...]).astype(BF16)
        lb = _lower_bound(lbl_ref)
        c1 = 0.5 * (1.0 - lb)
        worst_blocks = []
        for c0 in range(0, n, HGRN_COL_BLOCK):
            cs = slice(c0, c0 + HGRN_COL_BLOCK)

            def proj(branch):
                lo = branch * n + c0
                return jnp.dot(xn, win_ref[:, lo:lo + HGRN_COL_BLOCK], preferred_element_type=F32)

            q_s[r0:r0 + rp, cs] = _silu_from_half(proj(0))
            ct = c1[:, cs] * jnp.tanh(proj(1))
            k_s[r0:r0 + rp, cs] = c1[:, cs] - ct
            g = (lb[:, cs] + c1[:, cs]) + ct
            v_s[r0:r0 + rp, cs] = proj(2).astype(BF16)
            gs_s[r0:r0 + rp, cs] = _silu_from_half(proj(3))
            lg = jnp.maximum(jnp.log(g), LOG_FLOOR) * LOG2_E
            h1 = lg.astype(BF16)
            r1 = lg - h1.astype(F32)
            h2 = r1.astype(BF16)
            h3 = (r1 - h2.astype(F32)).astype(BF16)
            worst = jnp.zeros((1, HGRN_COL_BLOCK), F32)
            for c in range(chunks_per_part):
                rs = slice(c * CHUNK, (c + 1) * CHUNK)
                cum = jnp.dot(tri3, jnp.concatenate([h1[rs], h2[rs], h3[rs]], axis=0), preferred_element_type=F32)
                cum_s[r0 + c * CHUNK:r0 + (c + 1) * CHUNK, cs] = cum
                mid = cum[half - 1:half, :]
                worst = jnp.maximum(worst, jnp.maximum(-mid, mid - cum[CHUNK - 1:CHUNK, :]))
            worst_blocks.append(worst)
        return jnp.concatenate(worst_blocks, axis=-1)

    def back(p):
        r0 = p * rp
        y_ref[r0:r0 + rp, :] = _hgrn_back(x_ref[r0:r0 + rp, :], o_s[r0:r0 + rp, :], gs_s[r0:r0 + rp, :],
                                           og_ref, wout_ref, fg_ref)

    def head_update(h, base, att, qe_b, ke_b, d):
        sl = slice(h * DK_B, (h + 1) * DK_B)
        vh = v_s[pl.ds(base, CHUNK), sl]
        st = st_s[h]
        o = _nt(qe_b[:, sl], st.astype(BF16)) + jnp.dot(att.astype(BF16), vh, preferred_element_type=F32)
        st_s[h] = d[:, sl] * st + _tn(vh, ke_b[:, sl])
        o_s[pl.ds(base, CHUNK), sl] = o

    def chunk_common(base):
        cu = cum_s[pl.ds(base, CHUNK), :]
        tail = cum_s[pl.ds(base + CHUNK - 8, 8), :]
        clast = tail[7:8, :]
        qf = q_s[pl.ds(base, CHUNK), :]
        kf = k_s[pl.ds(base, CHUNK), :]
        qe = qf * jnp.exp2(cu)
        ke_b = (kf * jnp.exp2(clast - cu)).astype(BF16)
        return cu, qf, kf, qe, ke_b, jnp.exp2(clast)

    def fast_chunk(base):
        cu_l = cum_s[pl.ds(base, half), :]
        cu_r = cum_s[pl.ds(base + half, half), :]
        q_l, q_r = q_s[pl.ds(base, half), :], q_s[pl.ds(base + half, half), :]
        k_l, k_r = k_s[pl.ds(base, half), :], k_s[pl.ds(base + half, half), :]
        cmid = cu_l[half - 1:half, :]
        clast = cu_r[half - 1:half, :]
        bf = lambda z: z.astype(BF16)
        qe_l, qe_r = bf(q_l * jnp.exp2(cu_l)), bf(q_r * jnp.exp2(cu_r))
        ke_l, ke_r = bf(k_l * jnp.exp2(clast - cu_l)), bf(k_r * jnp.exp2(clast - cu_r))
        za_l = bf(k_l * jnp.exp2(cmid - cu_l))
        za_r = bf(q_r * jnp.exp2(cu_r - cmid))
        kl_l = bf(k_l * jnp.exp2(-cu_l))
        kl_r = bf(k_r * jnp.exp2(cmid - cu_r))
        qe_b = jnp.concatenate([qe_l, qe_r], axis=0)
        ke_b = jnp.concatenate([ke_l, ke_r], axis=0)
        za_b = jnp.concatenate([za_l, za_r], axis=0)
        kl_b = jnp.concatenate([kl_l, kl_r], axis=0)
        ql_b = jnp.concatenate([qe_l, za_r], axis=0)
        d = jnp.exp2(clast)
        cross = (row_c >= half) & (col_c < half)
        local = (row_c >= col_c) & ((row_c >= half) == (col_c >= half))
        for h in range(H_B):
            sl = slice(h * DK_B, (h + 1) * DK_B)
            pa = _nt(za_b[:, sl], za_b[:, sl])
            pb = _nt(ql_b[:, sl], kl_b[:, sl])
            att = jnp.where(cross, pa, jnp.where(local, pb, 0.0))
            head_update(h, base, att, qe_b, ke_b, d)

    def slow_chunk(c, carry):
        base = pl.multiple_of(c * CHUNK, CHUNK)
        cu_all, qf, kf, qe, ke_b, d = chunk_common(base)
        qe_b = qe.astype(BF16)
        xor_c = row_c ^ col_c
        row_l = lax.broadcasted_iota(jnp.int32, (CHUNK, LANES), 0)
        sub8 = lax.broadcasted_iota(jnp.int32, (8, LANES), 0)
        for h in range(H_B):
            sl = slice(h * DK_B, (h + 1) * DK_B)
            cu = cu_all[:, sl]
            qh = qf[:, sl]
            kh = kf[:, sl]

            def ref_row(r):
                grp = cu[(r // 8) * 8:(r // 8) * 8 + 8, :]
                return grp[r % 8:r % 8 + 1, :]

            att = jnp.zeros((CHUNK, CHUNK), F32)
            for lm in range(LOG_CHUNK):
                m = 1 << lm
                if lm >= 2:
                    cm = jnp.concatenate(
                        [jnp.broadcast_to(ref_row(blk * 2 * m + m - 1), (2 * m, LANES))
                         for blk in range(CHUNK // (2 * m))], axis=0)
                elif lm == 1:
                    cm = jnp.concatenate(
                        [jnp.where(sub8 < 4,
                                   jnp.broadcast_to(ref_row(blk * 8 + 1), (8, LANES)),
                                   jnp.broadcast_to(ref_row(blk * 8 + 5), (8, LANES)))
                         for blk in range(CHUNK // 8)], axis=0)
                else:
                    cm = jnp.where((row_l & 1) == 1, pltpu.roll(cu, 1, axis=0), cu)
                right = ((row_l >> lm) & 1) == 1
                z = (jnp.where(right, qh, kh) * jnp.exp2(-jnp.abs(cu - cm))).astype(BF16)
                pair = ((xor_c >> lm) == 1) & (((row_c >> lm) & 1) == 1)
                att = jnp.where(pair, _nt(z, z), att)
            att = jnp.where(row_c == col_c, _nt(qh.astype(BF16), kh.astype(BF16)), att)
            head_update(h, base, att, qe_b, ke_b, d)
        return carry

    worst = front(0)
    for p in range(1, HGRN_PARTS):
        worst = jnp.maximum(worst, front(p))
    for c in range(n_chunks):
        fast_chunk(c * CHUNK)
    for p in range(HGRN_PARTS):
        back(p)

    @pl.when(jnp.max(worst) > MAX_LOCAL_EXPONENT * LOG2_E)
    def _():
        st_s[...] = st0_s[...]
        lax.fori_loop(0, n_chunks, slow_chunk, 0)
        for p in range(HGRN_PARTS):
            back(p)

    @pl.when(i == pl.num_programs(1) - 1)
    def _():
        for h in range(H_B):
            st_ref[h] = st_s[h].T


def _hgrn_prompt(x1, gain, w_in, lbl, o_gain, w_out, f_gain, *, tt):
    nb, total_t, _ = x1.shape
    return pl.pallas_call(
        functools.partial(_hgrn_prompt_kernel, tt),
        grid=(nb, total_t // tt),
        in_specs=[pl.BlockSpec((None, tt, D_MODEL), lambda b, i: (b, i, 0)),
                  _const_spec((1, D_MODEL)),
                  _const_spec((D_MODEL, 4 * INNER_B)),
                  _const_spec((2, INNER_B)),
                  _const_spec((1, INNER_B)),
                  _const_spec((INNER_B, D_MODEL)),
                  _const_spec((1, D_MODEL))],
        out_specs=[pl.BlockSpec((None, tt, D_MODEL), lambda b, i: (b, i, 0)),
                   pl.BlockSpec((None, H_B, DK_B, DV_B), lambda b, i: (b, 0, 0, 0))],
        out_shape=[jax.ShapeDtypeStruct((nb, total_t, D_MODEL), F32),
                   jax.ShapeDtypeStruct((nb, H_B, DK_B, DV_B), F32)],
        scratch_shapes=[pltpu.VMEM((tt, INNER_B), F32),
                        pltpu.VMEM((tt, INNER_B), F32),
                        pltpu.VMEM((tt, INNER_B), BF16),
                        pltpu.VMEM((tt, INNER_B), F32),
                        pltpu.VMEM((tt, INNER_B), F32),
                        pltpu.VMEM((tt, INNER_B), F32),
                        pltpu.VMEM((H_B, DV_B, DK_B), F32),
                        pltpu.VMEM((H_B, DV_B, DK_B), F32)],
        compiler_params=pltpu.CompilerParams(dimension_semantics=("arbitrary", "arbitrary"),
                                             vmem_limit_bytes=VMEM_LIMIT),
        name="hgrn_prompt",
    )(x1, gain, w_in, lbl, o_gain, w_out, f_gain)


def _hgrn_decode_kernel(nb, tt, bstep, x_ref, gain_ref, win_ref, lbl_ref, og_ref, wout_ref, fg_ref, s0_ref,
                        y_ref, snew_ref, qe_s, ke_s, dd_s, v_s, oi_s, gs_s, x_s, o_s):
    j = pl.program_id(0)
    rows = nb * tt

    def to_batch_major(dst, val, t):
        for h in range(H_B):
            dst[h, pl.ds(t, nb, stride=tt), :] = val[:, h * LANES:(h + 1) * LANES]

    @pl.when(j == 0)
    def _():
        x = x_ref[...]
        q, kk, v, lg, gs = _hgrn_front(x, gain_ref, win_ref, lbl_ref)
        ones_blk = jnp.ones((DK_B, DK_B), BF16)

        def slab(arr, t):
            return arr[t * nb:(t + 1) * nb, :]

        cum = [slab(lg, 0)]
        for t in range(1, tt):
            cum.append(cum[-1] + slab(lg, t))
        dd_s[...] = jnp.zeros_like(dd_s)
        d = jnp.exp(cum[tt - 1])
        d_hi = d.astype(BF16).astype(F32)
        to_batch_major(dd_s, d_hi, 0)
        to_batch_major(dd_s, d - d_hi, 1)
        for t in range(tt):
            qt = slab(q, t)
            acc = jnp.zeros((nb, INNER_B), F32)
            for s in range(t + 1):
                prod = qt * slab(kk, s)
                if s < t:
                    prod = prod * jnp.exp(cum[t] - cum[s])
                pb = prod.astype(BF16)
                att = jnp.concatenate(
                    [jnp.dot(pb[:, h * DK_B:(h + 1) * DK_B], ones_blk, preferred_element_type=F32)
                     for h in range(H_B)], axis=-1)
                acc = acc + att * slab(v, s)
            to_batch_major(oi_s, acc, t)
            to_batch_major(qe_s, qt * jnp.exp(cum[t]), t)
            to_batch_major(ke_s, slab(kk, t) * jnp.exp(cum[tt - 1] - cum[t]), t)
            to_batch_major(v_s, slab(v, t), t)
            to_batch_major(gs_s, slab(gs, t), t)
            to_batch_major(x_s, slab(x, t), t)

    row1 = lax.broadcasted_iota(jnp.int32, (8, LANES), 0)
    lane2 = lax.broadcasted_iota(jnp.int32, (8, 2 * DV_B), 1)
    row2 = lax.broadcasted_iota(jnp.int32, (8, 2 * DV_B), 0)
    own = (row1 < tt, row1 >= tt)
    ones_rows = tuple(
        jnp.where((lane2 >= DV_B) & (row2 >= lo) & (row2 < lo + 2), 1.0, 0.0).astype(BF16) for lo in (tt, 0))

    def pair_body(pi, carry):
        off = pl.multiple_of((j * (bstep // 2) + pi) * 8, 8)
        for h in range(H_B):
            qe = qe_s[h, pl.ds(off, 8), :].astype(BF16)
            ke = ke_s[h, pl.ds(off, 8), :]
            dd = pltpu.roll(dd_s[h, pl.ds(off, 8), :], tt, axis=0)
            vv = v_s[h, pl.ds(off, 8), :]
            inter = []
            for e in range(2):
                s0 = s0_ref[2 * pi + e, h]
                inter.append(jnp.dot(qe, s0.astype(BF16), preferred_element_type=F32))
                lhs = jnp.where(own[e], ke, dd).astype(BF16)
                ve = jnp.where(own[e], vv, 0.0).astype(BF16)
                rhs = jnp.concatenate([ve, jnp.zeros((8, DV_B), BF16)], axis=-1) + ones_rows[e]
                upd = lax.dot_general(lhs, rhs, (((0,), (0,)), ((), ())),
                                      preferred_element_type=F32)
                snew_ref[2 * pi + e, h] = upd[:, DV_B:] * s0 + upd[:, :DV_B]
            o_s[pl.ds(off, 8), h * DV_B:(h + 1) * DV_B] = (
                jnp.where(own[0], inter[0], inter[1]) + oi_s[h, pl.ds(off, 8), :])
        return carry

    lax.fori_loop(0, bstep // 2, pair_body, 0, unroll=True)

    @pl.when(j == pl.num_programs(0) - 1)
    def _():
        xb = jnp.concatenate([x_s[h] for h in range(H_B)], axis=-1)
        gsb = jnp.concatenate([gs_s[h] for h in range(H_B)], axis=-1)
        y_ref[...] = _hgrn_back(xb, o_s[...], gsb, og_ref, wout_ref, fg_ref)


def _hgrn_decode(x1, s0, gain, w_in, lbl, o_gain, w_out, f_gain, *, nb, tt, bstep):
    assert 2 * tt == 8 and bstep % 2 == 0
    rows = nb * tt
    slab = pltpu.VMEM((H_B, rows, LANES), F32)
    return pl.pallas_call(
        functools.partial(_hgrn_decode_kernel, nb, tt, bstep),
        grid=(nb // bstep,),
        in_specs=[_const_spec((rows, D_MODEL)),
                  _const_spec((1, D_MODEL)),
                  _const_spec((D_MODEL, 4 * INNER_B)),
                  _const_spec((2, INNER_B)),
                  _const_spec((1, INNER_B)),
                  _const_spec((INNER_B, D_MODEL)),
                  _const_spec((1, D_MODEL)),
                  pl.BlockSpec((bstep, H_B, DK_B, DV_B), lambda j: (j, 0, 0, 0))],
        out_specs=[pl.BlockSpec((rows, D_MODEL), lambda j: (0, 0)),
                   pl.BlockSpec((bstep, H_B, DK_B, DV_B), lambda j: (j, 0, 0, 0))],
        out_shape=[jax.ShapeDtypeStruct((rows, D_MODEL), F32),
                   jax.ShapeDtypeStruct((nb, H_B, DK_B, DV_B), F32)],
        scratch_shapes=[slab, slab, slab, slab, slab, slab, slab,
                        pltpu.VMEM((rows, INNER_B), F32)],
        compiler_params=pltpu.CompilerParams(dimension_semantics=("arbitrary",),
                                             vmem_limit_bytes=VMEM_LIMIT),
        name="hgrn_decode",
    )(x1, gain, w_in, lbl, o_gain, w_out, f_gain, s0)


def _gate_weights(w_r, w_i):
    per = GATE_SUPER // BLK_W

    def block_diag(w):
        w4 = w.reshape(N_SUPER, per, BLK_W, BLK_W)
        eye = jnp.eye(per, dtype=w.dtype)
        return jnp.einsum('saij,ac->saicj', w4, eye).reshape(N_SUPER, GATE_SUPER, GATE_SUPER)

    dr, di = block_diag(w_r), block_diag(w_i)
    out = []
    for j, (lo, hi) in enumerate(GATE_K_RANGES):
        cols = slice(j * LANES, (j + 1) * LANES)
        out.append(jnp.concatenate([dr[:, lo:hi, cols], di[:, lo:hi, cols]], axis=-1).astype(BF16))
    return out


def kernel(x_prompt, x_sample, state_lru_h, state_lru_conv, state_hgrn, norm_gain, a_w_in, a_conv_w, a_conv_b,
           a_w_r, a_b_r, a_w_i, a_b_i, a_lambda, a_w_out, b_w_in, b_lb_logits, b_o_gain, b_w_out, final_gain):
    assert norm_gain.shape[0] == 2 and a_w_in.shape[0] == 1 and b_w_in.shape[0] == 1
    pb, pt, _ = x_prompt.shape
    sb, st, _ = x_sample.shape
    halo_t = CONV_W - 1

    row = lambda p: p.reshape(1, -1)
    lru_col_scale = np.concatenate([np.ones((W_LRU,), np.float32), np.full((W_LRU,), 0.5, np.float32)])
    hgrn_col_scale = np.concatenate([np.full((2 * INNER_B,), 0.5, np.float32), np.ones((INNER_B,), np.float32),
                                     np.full((INNER_B,), 0.5, np.float32)])
    lru_params = (row(norm_gain[0]), (a_w_in[0] * lru_col_scale).astype(BF16), a_conv_w[0],
                  row(a_conv_b[0]), *_gate_weights(a_w_r[0], a_w_i[0]), row(a_b_r[0]),
                  row(a_b_i[0]), row(a_lambda[0]), a_w_out[0].astype(BF16))
    hgrn_params = (row(norm_gain[1]), (b_w_in[0] * hgrn_col_scale).astype(BF16), b_lb_logits, row(b_o_gain[0]),
                   b_w_out[0].astype(BF16), row(final_gain))

    x1p, hp, bufp = _lru_layer(x_prompt, jnp.zeros((pb, W_LRU), F32), jnp.zeros((halo_t * pb, W_LRU), F32),
                               *lru_params, nb=pb, tt=LRU_TILE_STEPS, batch_major=True)
    y_prompt, sp = _hgrn_prompt(x1p, *hgrn_params, tt=HGRN_TILE_ROWS)
    bufp = bufp.reshape(halo_t, pb, W_LRU).transpose(1, 0, 2)

    xs = x_sample.transpose(1, 0, 2).reshape(st * sb, D_MODEL)
    bufs0 = state_lru_conv[0].transpose(1, 0, 2).reshape(halo_t * sb, W_LRU)
    x1s, hs, bufs = _lru_layer(xs, state_lru_h[0], bufs0, *lru_params, nb=sb, tt=st, batch_major=False)
    ys, ss = _hgrn_decode(x1s, state_hgrn[0], *hgrn_params, nb=sb, tt=st, bstep=DECODE_SEQS_PER_STEP)
    bufs = bufs.reshape(halo_t, sb, W_LRU).transpose(1, 0, 2)

    return (y_prompt, ys.reshape(sb, st, D_MODEL), hp[None], bufp[None], sp[None],
            hs[None], bufs[None], ss[None])
```

```python
import functools

import numpy as np

import jax
import jax.numpy as jnp
from jax import lax
from jax.experimental import pallas as pl
from jax.experimental.pallas import tpu as pltpu

F32 = jnp.float32
BF16 = jnp.bfloat16

D_MODEL = 1024
W_LRU = 1536
N_BLK = 16
BLK_W = W_LRU // N_BLK
CONV_W = 4
LRU_C = 8.0
H_B = 8
DK_B = 128
DV_B = 128
INNER_B = H_B * DK_B
EPS = 1e-6

LANES = 128
GATE_SUPER = 384
N_SUPER = W_LRU // GATE_SUPER
LRU_PARTS = 1
HGRN_PARTS = 1
LRU_CHANNEL_BLOCK = 768
LRU_TILE_STEPS = 64
HGRN_COL_BLOCK = 512
HGRN_TILE_ROWS = 512
DECODE_SEQS_PER_STEP = 8
CHUNK = 64
LOG_CHUNK = 6
VMEM_LIMIT = 60 * 1024 * 1024
MAX_LOCAL_EXPONENT = 75.0
LOG2_E = 1.4426950408889634
LOG_FLOOR = -1e4


def _silu_from_half(hx):
    return hx + hx * jnp.tanh(hx)


def _rms_scale(x, gain):
    ms = jnp.mean(x * x, axis=-1, keepdims=True)
    return x * lax.rsqrt(ms + EPS) * gain


def _const_spec(shape):
    zeros = (0,) * len(shape)
    return pl.BlockSpec(shape, lambda *_: zeros, pipeline_mode=pl.Buffered(1))


def _gate_k_ranges():
    ranges = []
    for j in range(GATE_SUPER // LANES):
        first_blk = (j * LANES) // BLK_W
        last_blk = ((j + 1) * LANES - 1) // BLK_W
        lo = (first_blk * BLK_W) // LANES * LANES
        hi = -(-((last_blk + 1) * BLK_W) // LANES) * LANES
        ranges.append((lo, hi))
    return ranges


GATE_K_RANGES = _gate_k_ranges()


def _lru_kernel(nb, tt, batch_major, x_ref, gain_ref, win_ref, cw_ref, cb_ref, wg0_ref, wg1_ref, wg2_ref,
                br_ref, bi_ref, lam_ref, wout_ref, h0_ref, buf0_ref, x1_ref, hlast_ref, bufout_ref, *scratch):
    wg_refs = (wg0_ref, wg1_ref, wg2_ref)
    if batch_major:
        xs_s, os_s, xbuf_s, a_s, b_s, g_s, hc_s = scratch
    else:
        xbuf_s, a_s, b_s, g_s, hc_s = scratch
    rows = nb * tt
    halo = (CONV_W - 1) * nb
    n_lane_tiles = D_MODEL // LANES
    tp = tt // LRU_PARTS
    rp = nb * tp

    @pl.when(pl.program_id(0) == 0)
    def _():
        hc_s[...] = h0_ref[...]
        xbuf_s[0:halo, :] = buf0_ref[...]

    if batch_major:
        for b in range(nb):
            for j in range(n_lane_tiles):
                xs_s[j, pl.ds(b, tt, stride=nb), :] = x_ref[b, :, j * LANES:(j + 1) * LANES]

    def load_x(p):
        if batch_major:
            return jnp.concatenate([xs_s[j, p * rp:(p + 1) * rp, :] for j in range(n_lane_tiles)], axis=-1)
        return x_ref[p * rp:(p + 1) * rp, :]

    nlam = -lam_ref[...]
    softplus = jnp.maximum(nlam, 0.0) + jnp.log1p(jnp.exp(-jnp.abs(nlam)))
    half_rate = (-0.5 * LRU_C) * softplus
    half_cw = 0.5 * cw_ref[...]
    half_cb = 0.5 * cb_ref[...]
    half_br = 0.5 * br_ref[...]
    half_bi = 0.5 * bi_ref[...]

    def pre(p):
        r0 = p * rp
        xn = _rms_scale(load_x(p), gain_ref[...]).astype(BF16)
        for c0 in range(0, W_LRU, LRU_CHANNEL_BLOCK):
            cs = slice(c0, c0 + LRU_CHANNEL_BLOCK)
            xbuf_s[halo + r0:halo + r0 + rp, cs] = jnp.dot(xn, win_ref[:, cs], preferred_element_type=F32)
            xh = half_cb[:, cs] + half_cw[0:1, cs] * xbuf_s[r0:r0 + rp, cs]
            for k in range(1, CONV_W):
                xh = xh + half_cw[k:k + 1, cs] * xbuf_s[r0 + k * nb:r0 + k * nb + rp, cs]
            xhb = xh.astype(BF16)
            r_parts, i_parts = [], []
            for s in range(LRU_CHANNEL_BLOCK // GATE_SUPER):
                for (lo, hi), wg_ref in zip(GATE_K_RANGES, wg_refs):
                    gsi = jnp.dot(xhb[:, s * GATE_SUPER + lo:s * GATE_SUPER + hi],
                                  wg_ref[c0 // GATE_SUPER + s], preferred_element_type=F32)
                    r_parts.append(gsi[:, :LANES])
                    i_parts.append(gsi[:, LANES:])
            tr = jnp.tanh(jnp.concatenate(r_parts, axis=-1) + half_br[:, cs])
            ti = jnp.tanh(jnp.concatenate(i_parts, axis=-1) + half_bi[:, cs])
            log_a = half_rate[:, cs] + half_rate[:, cs] * tr
            a = jnp.exp(log_a)
            one_m_a2 = -jnp.tanh(log_a) * (a * a + 1.0)
            a_s[r0:r0 + rp, cs] = a
            root = jnp.where(one_m_a2 > 0.0, one_m_a2 * lax.rsqrt(one_m_a2), 0.0)
            b_s[r0:r0 + rp, cs] = root * (xh * (1.0 + ti))
        for c0 in range(0, W_LRU, LRU_CHANNEL_BLOCK):
            cs = slice(c0, c0 + LRU_CHANNEL_BLOCK)
            hg = jnp.dot(xn, win_ref[:, W_LRU + c0:W_LRU + c0 + LRU_CHANNEL_BLOCK], preferred_element_type=F32)
            g_s[r0:r0 + rp, cs] = _silu_from_half(hg)

    def scan(p, h):
        for t in range(p * tp, (p + 1) * tp):
            h = a_s[t * nb:(t + 1) * nb, :] * h + b_s[t * nb:(t + 1) * nb, :]
            b_s[t * nb:(t + 1) * nb, :] = h
        return h

    def post(p):
        r0 = p * rp
        out = load_x(p)
        for c0 in range(0, W_LRU, LRU_CHANNEL_BLOCK):
            cs = slice(c0, c0 + LRU_CHANNEL_BLOCK)
            y = (b_s[r0:r0 + rp, cs] * g_s[r0:r0 + rp, cs]).astype(BF16)
            out = out + jnp.dot(y, wout_ref[cs, :], preferred_element_type=F32)
        if batch_major:
            for j in range(n_lane_tiles):
                os_s[j, r0:r0 + rp, :] = out[:, j * LANES:(j + 1) * LANES]
            for b in range(nb):
                x1_ref[b, p * tp:(p + 1) * tp, :] = jnp.concatenate(
                    [os_s[j, pl.ds(r0 + b, tp, stride=nb), :] for j in range(n_lane_tiles)], axis=-1)
        else:
            x1_ref[r0:r0 + rp, :] = out

    for p in range(LRU_PARTS):
        pre(p)
    new_halo = xbuf_s[rows:rows + halo, :]
    bufout_ref[...] = new_halo
    xbuf_s[0:halo, :] = new_halo
    h = hc_s[...]
    for p in range(LRU_PARTS):
        h = scan(p, h)
        post(p)
    hc_s[...] = h
    hlast_ref[...] = h


def _lru_layer(x, h0, buf0, gain, w_in, conv_w, conv_b, wg0, wg1, wg2, b_r, b_i, lam, w_out, *, nb, tt, batch_major):
    if batch_major:
        total_t = x.shape[1]
        x_spec = pl.BlockSpec((nb, tt, D_MODEL), lambda i: (0, i, 0))
        x_shape = jax.ShapeDtypeStruct((nb, total_t, D_MODEL), F32)
    else:
        total_t = x.shape[0] // nb
        x_spec = pl.BlockSpec((nb * tt, D_MODEL), lambda i: (i, 0))
        x_shape = jax.ShapeDtypeStruct((nb * total_t, D_MODEL), F32)
    rows = nb * tt
    halo = (CONV_W - 1) * nb
    scratch = []
    if batch_major:
        scratch += [pltpu.VMEM((D_MODEL // LANES, rows, LANES), F32),
                    pltpu.VMEM((D_MODEL // LANES, rows, LANES), F32)]
    scratch += [pltpu.VMEM((rows + halo, W_LRU), F32),
                pltpu.VMEM((rows, W_LRU), F32),
                pltpu.VMEM((rows, W_LRU), F32),
                pltpu.VMEM((rows, W_LRU), F32),
                pltpu.VMEM((nb, W_LRU), F32)]
    return pl.pallas_call(
        functools.partial(_lru_kernel, nb, tt, batch_major),
        grid=(total_t // tt,),
        in_specs=[x_spec,
                  _const_spec((1, D_MODEL)),
                  _const_spec((D_MODEL, 2 * W_LRU)),
                  _const_spec((CONV_W, W_LRU)),
                  _const_spec((1, W_LRU)),
                  *[_const_spec((N_SUPER, hi - lo, 2 * LANES)) for lo, hi in GATE_K_RANGES],
                  _const_spec((1, W_LRU)),
                  _const_spec((1, W_LRU)),
                  _const_spec((1, W_LRU)),
                  _const_spec((W_LRU, D_MODEL)),
                  _const_spec((nb, W_LRU)),
                  _const_spec((halo, W_LRU))],
        out_specs=[x_spec,
                   pl.BlockSpec((nb, W_LRU), lambda i: (0, 0)),
                   pl.BlockSpec((halo, W_LRU), lambda i: (0, 0))],
        out_shape=[x_shape,
                   jax.ShapeDtypeStruct((nb, W_LRU), F32),
                   jax.ShapeDtypeStruct((halo, W_LRU), F32)],
        scratch_shapes=scratch,
        compiler_params=pltpu.CompilerParams(dimension_semantics=("arbitrary",),
                                             vmem_limit_bytes=VMEM_LIMIT),
        name="lru_layer_bm" if batch_major else "lru_layer_tm",
    )(x, gain, w_in, conv_w, conv_b, wg0, wg1, wg2, b_r, b_i, lam, w_out, h0, buf0)


def _lower_bound(lbl_ref):
    l0 = lbl_ref[0:1, :]
    l1 = lbl_ref[1:2, :]
    mx = jnp.maximum(l0, l1)
    e0 = jnp.exp(l0 - mx)
    e1 = jnp.exp(l1 - mx)
    return e1 / (e0 + e1)


def _hgrn_front(x, gain_ref, win_ref, lbl_ref):
    xn = _rms_scale(x, gain_ref[...]).astype(BF16)
    n = INNER_B
    q = _silu_from_half(jnp.dot(xn, win_ref[:, 0:n], preferred_element_type=F32))
    tf = jnp.tanh(jnp.dot(xn, win_ref[:, n:2 * n], preferred_element_type=F32))
    v = jnp.dot(xn, win_ref[:, 2 * n:3 * n], preferred_element_type=F32)
    gs = _silu_from_half(jnp.dot(xn, win_ref[:, 3 * n:4 * n], preferred_element_type=F32))
    lb = _lower_bound(lbl_ref)
    c1 = 0.5 * (1.0 - lb)
    ct = c1 * tf
    kk = c1 - ct
    g = (lb + c1) + ct
    lg = jnp.maximum(jnp.log(g), LOG_FLOOR)
    return q, kk, v, lg, gs


def _hgrn_back(x, o, gs, og_ref, wout_ref, fg_ref):
    parts = []
    for h in range(H_B):
        oh = o[:, h * DV_B:(h + 1) * DV_B]
        ms = jnp.mean(oh * oh, axis=-1, keepdims=True)
        parts.append(oh * lax.rsqrt(ms + EPS))
    on = jnp.concatenate(parts, axis=-1) * og_ref[...]
    y = (on * gs).astype(BF16)
    x2 = x + jnp.dot(y, wout_ref[...], preferred_element_type=F32)
    return _rms_scale(x2, fg_ref[...])


def _nt(a, b):
    return lax.dot_general(a, b, (((1,), (1,)), ((), ())), preferred_element_type=F32)


def _tn(a, b):
    return lax.dot_general(a, b, (((0,), (0,)), ((), ())), preferred_element_type=F32)


def _hgrn_prompt_kernel(tt, x_ref, gain_ref, win_ref, lbl_ref, og_ref, wout_ref, fg_ref,
                        y_ref, st_ref, q_s, k_s, v_s, cum_s, gs_s, o_s, st_s, st0_s):
    i = pl.program_id(1)
    n_chunks = tt // CHUNK
    half = CHUNK // 2
    rp = tt // HGRN_PARTS
    chunks_per_part = rp // CHUNK

    @pl.when(i == 0)
    def _():
        st_s[...] = jnp.zeros_like(st_s)

    st0_s[...] = st_s[...]

    row_c = lax.broadcasted_iota(jnp.int32, (CHUNK, CHUNK), 0)
    col_c = lax.broadcasted_iota(jnp.int32, (CHUNK, CHUNK), 1)
    tri = (row_c >= col_c).astype(BF16)
    tri3 = jnp.concatenate([tri, tri, tri], axis=-1)

    def front(p):
        r0 = p * rp
        n = INNER_B
        xn = _rms_scale(x_ref[r0:r0 + rp, :], gain_ref[...]).astype(BF16)
        lb = _lower_bound(lbl_ref)
        c1 = 0.5 * (1.0 - lb)
        worst_blocks = []
        for c0 in range(0, n, HGRN_COL_BLOCK):
            cs = slice(c0, c0 + HGRN_COL_BLOCK)

            def proj(branch):
                lo = branch * n + c0
                return jnp.dot(xn, win_ref[:, lo:lo + HGRN_COL_BLOCK], preferred_element_type=F32)

            q_s[r0:r0 + rp, cs] = _silu_from_half(proj(0))
            ct = c1[:, cs] * jnp.tanh(proj(1))
            k_s[r0:r0 + rp, cs] = c1[:, cs] - ct
            g = (lb[:, cs] + c1[:, cs]) + ct
            v_s[r0:r0 + rp, cs] = proj(2).astype(BF16)
            gs_s[r0:r0 + rp, cs] = _silu_from_half(proj(3))
            lg = jnp.maximum(jnp.log(g), LOG_FLOOR) * LOG2_E
            h1 = lg.astype(BF16)
            r1 = lg - h1.astype(F32)
            h2 = r1.astype(BF16)
            h3 = (r1 - h2.astype(F32)).astype(BF16)
            worst = jnp.zeros((1, HGRN_COL_BLOCK), F32)
            for c in range(chunks_per_part):
                rs = slice(c * CHUNK, (c + 1) * CHUNK)
                cum = jnp.dot(tri3, jnp.concatenate([h1[rs], h2[rs], h3[rs]], axis=0), preferred_element_type=F32)
                cum_s[r0 + c * CHUNK:r0 + (c + 1) * CHUNK, cs] = cum
                mid = cum[half - 1:half, :]
                worst = jnp.maximum(worst, jnp.maximum(-mid, mid - cum[CHUNK - 1:CHUNK, :]))
            worst_blocks.append(worst)
        return jnp.concatenate(worst_blocks, axis=-1)

    def back(p):
        r0 = p * rp
        y_ref[r0:r0 + rp, :] = _hgrn_back(x_ref[r0:r0 + rp, :], o_s[r0:r0 + rp, :], gs_s[r0:r0 + rp, :],
                                           og_ref, wout_ref, fg_ref)

    def head_update(h, base, att, qe_b, ke_b, d):
        sl = slice(h * DK_B, (h + 1) * DK_B)
        vh = v_s[pl.ds(base, CHUNK), sl]
        st = st_s[h]
        o = _nt(qe_b[:, sl], st.astype(BF16)) + jnp.dot(att.astype(BF16), vh, preferred_element_type=F32)
        st_s[h] = d[:, sl] * st + _tn(vh, ke_b[:, sl])
        o_s[pl.ds(base, CHUNK), sl] = o

    def chunk_common(base):
        cu = cum_s[pl.ds(base, CHUNK), :]
        tail = cum_s[pl.ds(base + CHUNK - 8, 8), :]
        clast = tail[7:8, :]
        qf = q_s[pl.ds(base, CHUNK), :]
        kf = k_s[pl.ds(base, CHUNK), :]
        qe = qf * jnp.exp2(cu)
        ke_b = (kf * jnp.exp2(clast - cu)).astype(BF16)
        return cu, qf, kf, qe, ke_b, jnp.exp2(clast)

    def fast_chunk(base):
        cu_l = cum_s[pl.ds(base, half), :]
        cu_r = cum_s[pl.ds(base + half, half), :]
        q_l, q_r = q_s[pl.ds(base, half), :], q_s[pl.ds(base + half, half), :]
        k_l, k_r = k_s[pl.ds(base, half), :], k_s[pl.ds(base + half, half), :]
        cmid = cu_l[half - 1:half, :]
        clast = cu_r[half - 1:half, :]
        bf = lambda z: z.astype(BF16)
        qe_l, qe_r = bf(q_l * jnp.exp2(cu_l)), bf(q_r * jnp.exp2(cu_r))
        ke_l, ke_r = bf(k_l * jnp.exp2(clast - cu_l)), bf(k_r * jnp.exp2(clast - cu_r))
        za_l = bf(k_l * jnp.exp2(cmid - cu_l))
        za_r = bf(q_r * jnp.exp2(cu_r - cmid))
        kl_l = bf(k_l * jnp.exp2(-cu_l))
        kl_r = bf(k_r * jnp.exp2(cmid - cu_r))
        qe_b = jnp.concatenate([qe_l, qe_r], axis=0)
        ke_b = jnp.concatenate([ke_l, ke_r], axis=0)
        za_b = jnp.concatenate([za_l, za_r], axis=0)
        kl_b = jnp.concatenate([kl_l, kl_r], axis=0)
        ql_b = jnp.concatenate([qe_l, za_r], axis=0)
        d = jnp.exp2(clast)
        cross = (row_c >= half) & (col_c < half)
        local = (row_c >= col_c) & ((row_c >= half) == (col_c >= half))
        for h in range(H_B):
            sl = slice(h * DK_B, (h + 1) * DK_B)
            pa = _nt(za_b[:, sl], za_b[:, sl])
            pb = _nt(ql_b[:, sl], kl_b[:, sl])
            att = jnp.where(cross, pa, jnp.where(local, pb, 0.0))
            head_update(h, base, att, qe_b, ke_b, d)

    def slow_chunk(c, carry):
        base = pl.multiple_of(c * CHUNK, CHUNK)
        cu_all, qf, kf, qe, ke_b, d = chunk_common(base)
        qe_b = qe.astype(BF16)
        xor_c = row_c ^ col_c
        row_l = lax.broadcasted_iota(jnp.int32, (CHUNK, LANES), 0)
        sub8 = lax.broadcasted_iota(jnp.int32, (8, LANES), 0)
        for h in range(H_B):
            sl = slice(h * DK_B, (h + 1) * DK_B)
            cu = cu_all[:, sl]
            qh = qf[:, sl]
            kh = kf[:, sl]

            def ref_row(r):
                grp = cu[(r // 8) * 8:(r // 8) * 8 + 8, :]
                return grp[r % 8:r % 8 + 1, :]

            att = jnp.zeros((CHUNK, CHUNK), F32)
            for lm in range(LOG_CHUNK):
                m = 1 << lm
                if lm >= 2:
                    cm = jnp.concatenate(
                        [jnp.broadcast_to(ref_row(blk * 2 * m + m - 1), (2 * m, LANES))
                         for blk in range(CHUNK // (2 * m))], axis=0)
                elif lm == 1:
                    cm = jnp.concatenate(
                        [jnp.where(sub8 < 4,
                                   jnp.broadcast_to(ref_row(blk * 8 + 1), (8, LANES)),
                                   jnp.broadcast_to(ref_row(blk * 8 + 5), (8, LANES)))
                         for blk in range(CHUNK // 8)], axis=0)
                else:
                    cm = jnp.where((row_l & 1) == 1, pltpu.roll(cu, 1, axis=0), cu)
                right = ((row_l >> lm) & 1) == 1
                z = (jnp.where(right, qh, kh) * jnp.exp2(-jnp.abs(cu - cm))).astype(BF16)
                pair = ((xor_c >> lm) == 1) & (((row_c >> lm) & 1) == 1)
                att = jnp.where(pair, _nt(z, z), att)
            att = jnp.where(row_c == col_c, _nt(qh.astype(BF16), kh.astype(BF16)), att)
            head_update(h, base, att, qe_b, ke_b, d)
        return carry

    worst = front(0)
    for p in range(1, HGRN_PARTS):
        worst = jnp.maximum(worst, front(p))
    for c in range(n_chunks):
        fast_chunk(c * CHUNK)
    for p in range(HGRN_PARTS):
        back(p)

    @pl.when(jnp.max(worst) > MAX_LOCAL_EXPONENT * LOG2_E)
    def _():
        st_s[...] = st0_s[...]
        lax.fori_loop(0, n_chunks, slow_chunk, 0)
        for p in range(HGRN_PARTS):
            back(p)

    @pl.when(i == pl.num_programs(1) - 1)
    def _():
        for h in range(H_B):
            st_ref[h] = st_s[h].T


def _hgrn_prompt(x1, gain, w_in, lbl, o_gain, w_out, f_gain, *, tt):
    nb, total_t, _ = x1.shape
    return pl.pallas_call(
        functools.partial(_hgrn_prompt_kernel, tt),
        grid=(nb, total_t // tt),
        in_specs=[pl.BlockSpec((None, tt, D_MODEL), lambda b, i: (b, i, 0)),
                  _const_spec((1, D_MODEL)),
                  _const_spec((D_MODEL, 4 * INNER_B)),
                  _const_spec((2, INNER_B)),
                  _const_spec((1, INNER_B)),
                  _const_spec((INNER_B, D_MODEL)),
                  _const_spec((1, D_MODEL))],
        out_specs=[pl.BlockSpec((None, tt, D_MODEL), lambda b, i: (b, i, 0)),
                   pl.BlockSpec((None, H_B, DK_B, DV_B), lambda b, i: (b, 0, 0, 0))],
        out_shape=[jax.ShapeDtypeStruct((nb, total_t, D_MODEL), F32),
                   jax.ShapeDtypeStruct((nb, H_B, DK_B, DV_B), F32)],
        scratch_shapes=[pltpu.VMEM((tt, INNER_B), F32),
                        pltpu.VMEM((tt, INNER_B), F32),
                        pltpu.VMEM((tt, INNER_B), BF16),
                        pltpu.VMEM((tt, INNER_B), F32),
                        pltpu.VMEM((tt, INNER_B), F32),
                        pltpu.VMEM((tt, INNER_B), F32),
                        pltpu.VMEM((H_B, DV_B, DK_B), F32),
                        pltpu.VMEM((H_B, DV_B, DK_B), F32)],
        compiler_params=pltpu.CompilerParams(dimension_semantics=("arbitrary", "arbitrary"),
                                             vmem_limit_bytes=VMEM_LIMIT),
        name="hgrn_prompt",
    )(x1, gain, w_in, lbl, o_gain, w_out, f_gain)


def _hgrn_decode_kernel(nb, tt, bstep, x_ref, gain_ref, win_ref, lbl_ref, og_ref, wout_ref, fg_ref, s0_ref,
                        y_ref, snew_ref, qe_s, ke_s, dd_s, v_s, oi_s, gs_s, x_s, o_s):
    j = pl.program_id(0)
    rows = nb * tt

    def to_batch_major(dst, val, t):
        for h in range(H_B):
            dst[h, pl.ds(t, nb, stride=tt), :] = val[:, h * LANES:(h + 1) * LANES]

    @pl.when(j == 0)
    def _():
        x = x_ref[...]
        q, kk, v, lg, gs = _hgrn_front(x, gain_ref, win_ref, lbl_ref)
        ones_blk = jnp.ones((DK_B, DK_B), BF16)

        def slab(arr, t):
            return arr[t * nb:(t + 1) * nb, :]

        cum = [slab(lg, 0)]
        for t in range(1, tt):
            cum.append(cum[-1] + slab(lg, t))
        dd_s[...] = jnp.zeros_like(dd_s)
        d = jnp.exp(cum[tt - 1])
        d_hi = d.astype(BF16).astype(F32)
        to_batch_major(dd_s, d_hi, 0)
        to_batch_major(dd_s, d - d_hi, 1)
        for t in range(tt):
            qt = slab(q, t)
            acc = jnp.zeros((nb, INNER_B), F32)
            for s in range(t + 1):
                prod = qt * slab(kk, s)
                if s < t:
                    prod = prod * jnp.exp(cum[t] - cum[s])
                pb = prod.astype(BF16)
                att = jnp.concatenate(
                    [jnp.dot(pb[:, h * DK_B:(h + 1) * DK_B], ones_blk, preferred_element_type=F32)
                     for h in range(H_B)], axis=-1)
                acc = acc + att * slab(v, s)
            to_batch_major(oi_s, acc, t)
            to_batch_major(qe_s, qt * jnp.exp(cum[t]), t)
            to_batch_major(ke_s, slab(kk, t) * jnp.exp(cum[tt - 1] - cum[t]), t)
            to_batch_major(v_s, slab(v, t), t)
            to_batch_major(gs_s, slab(gs, t), t)
            to_batch_major(x_s, slab(x, t), t)

    row1 = lax.broadcasted_iota(jnp.int32, (8, LANES), 0)
    lane2 = lax.broadcasted_iota(jnp.int32, (8, 2 * DV_B), 1)
    row2 = lax.broadcasted_iota(jnp.int32, (8, 2 * DV_B), 0)
    own = (row1 < tt, row1 >= tt)
    ones_rows = tuple(
        jnp.where((lane2 >= DV_B) & (row2 >= lo) & (row2 < lo + 2), 1.0, 0.0).astype(BF16) for lo in (tt, 0))

    def pair_body(pi, carry):
        off = pl.multiple_of((j * (bstep // 2) + pi) * 8, 8)
        for h in range(H_B):
            qe = qe_s[h, pl.ds(off, 8), :].astype(BF16)
            ke = ke_s[h, pl.ds(off, 8), :]
            dd = pltpu.roll(dd_s[h, pl.ds(off, 8), :], tt, axis=0)
            vv = v_s[h, pl.ds(off, 8), :]
            inter = []
            for e in range(2):
                s0 = s0_ref[2 * pi + e, h]
                inter.append(jnp.dot(qe, s0.astype(BF16), preferred_element_type=F32))
                lhs = jnp.where(own[e], ke, dd).astype(BF16)
                ve = jnp.where(own[e], vv, 0.0).astype(BF16)
                rhs = jnp.concatenate([ve, jnp.zeros((8, DV_B), BF16)], axis=-1) + ones_rows[e]
                upd = lax.dot_general(lhs, rhs, (((0,), (0,)), ((), ())),
                                      preferred_element_type=F32)
                snew_ref[2 * pi + e, h] = upd[:, DV_B:] * s0 + upd[:, :DV_B]
            o_s[pl.ds(off, 8), h * DV_B:(h + 1) * DV_B] = (
                jnp.where(own[0], inter[0], inter[1]) + oi_s[h, pl.ds(off, 8), :])
        return carry

    lax.fori_loop(0, bstep // 2, pair_body, 0, unroll=True)

    @pl.when(j == pl.num_programs(0) - 1)
    def _():
        xb = jnp.concatenate([x_s[h] for h in range(H_B)], axis=-1)
        gsb = jnp.concatenate([gs_s[h] for h in range(H_B)], axis=-1)
        y_ref[...] = _hgrn_back(xb, o_s[...], gsb, og_ref, wout_ref, fg_ref)


def _hgrn_decode(x1, s0, gain, w_in, lbl, o_gain, w_out, f_gain, *, nb, tt, bstep):
    assert 2 * tt == 8 and bstep % 2 == 0
    rows = nb * tt
    slab = pltpu.VMEM((H_B, rows, LANES), F32)
    return pl.pallas_call(
        functools.partial(_hgrn_decode_kernel, nb, tt, bstep),
        grid=(nb // bstep,),
        in_specs=[_const_spec((rows, D_MODEL)),
                  _const_spec((1, D_MODEL)),
                  _const_spec((D_MODEL, 4 * INNER_B)),
                  _const_spec((2, INNER_B)),
                  _const_spec((1, INNER_B)),
                  _const_spec((INNER_B, D_MODEL)),
                  _const_spec((1, D_MODEL)),
                  pl.BlockSpec((bstep, H_B, DK_B, DV_B), lambda j: (j, 0, 0, 0))],
        out_specs=[pl.BlockSpec((rows, D_MODEL), lambda j: (0, 0)),
                   pl.BlockSpec((bstep, H_B, DK_B, DV_B), lambda j: (j, 0, 0, 0))],
        out_shape=[jax.ShapeDtypeStruct((rows, D_MODEL), F32),
                   jax.ShapeDtypeStruct((nb, H_B, DK_B, DV_B), F32)],
        scratch_shapes=[slab, slab, slab, slab, slab, slab, slab,
                        pltpu.VMEM((rows, INNER_B), F32)],
        compiler_params=pltpu.CompilerParams(dimension_semantics=("arbitrary",),
                                             vmem_limit_bytes=VMEM_LIMIT),
        name="hgrn_decode",
    )(x1, gain, w_in, lbl, o_gain, w_out, f_gain, s0)


def _gate_weights(w_r, w_i):
    per = GATE_SUPER // BLK_W

    def block_diag(w):
        w4 = w.reshape(N_SUPER, per, BLK_W, BLK_W)
        eye = jnp.eye(per, dtype=w.dtype)
        return jnp.einsum('saij,ac->saicj', w4, eye).reshape(N_SUPER, GATE_SUPER, GATE_SUPER)

    dr, di = block_diag(w_r), block_diag(w_i)
    out = []
    for j, (lo, hi) in enumerate(GATE_K_RANGES):
        cols = slice(j * LANES, (j + 1) * LANES)
        out.append(jnp.concatenate([dr[:, lo:hi, cols], di[:, lo:hi, cols]], axis=-1).astype(BF16))
    return out


def kernel(x_prompt, x_sample, state_lru_h, state_lru_conv, state_hgrn, norm_gain, a_w_in, a_conv_w, a_conv_b,
           a_w_r, a_b_r, a_w_i, a_b_i, a_lambda, a_w_out, b_w_in, b_lb_logits, b_o_gain, b_w_out, final_gain):
    assert norm_gain.shape[0] == 2 and a_w_in.shape[0] == 1 and b_w_in.shape[0] == 1
    pb, pt, _ = x_prompt.shape
    sb, st, _ = x_sample.shape
    halo_t = CONV_W - 1

    row = lambda p: p.reshape(1, -1)
    lru_col_scale = np.concatenate([np.ones((W_LRU,), np.float32), np.full((W_LRU,), 0.5, np.float32)])
    hgrn_col_scale = np.concatenate([np.full((2 * INNER_B,), 0.5, np.float32), np.ones((INNER_B,), np.float32),
                                     np.full((INNER_B,), 0.5, np.float32)])
    lru_params = (row(norm_gain[0]), (a_w_in[0] * lru_col_scale).astype(BF16), a_conv_w[0],
                  row(a_conv_b[0]), *_gate_weights(a_w_r[0], a_w_i[0]), row(a_b_r[0]),
                  row(a_b_i[0]), row(a_lambda[0]), a_w_out[0].astype(BF16))
    hgrn_params = (row(norm_gain[1]), (b_w_in[0] * hgrn_col_scale).astype(BF16), b_lb_logits, row(b_o_gain[0]),
                   b_w_out[0].astype(BF16), row(final_gain))

    x1p, hp, bufp = _lru_layer(x_prompt, jnp.zeros((pb, W_LRU), F32), jnp.zeros((halo_t * pb, W_LRU), F32),
                               *lru_params, nb=pb, tt=LRU_TILE_STEPS, batch_major=True)
    y_prompt, sp = _hgrn_prompt(x1p, *hgrn_params, tt=HGRN_TILE_ROWS)
    bufp = bufp.reshape(halo_t, pb, W_LRU).transpose(1, 0, 2)

    xs = x_sample.transpose(1, 0, 2).reshape(st * sb, D_MODEL)
    bufs0 = state_lru_conv[0].transpose(1, 0, 2).reshape(halo_t * sb, W_LRU)
    x1s, hs, bufs = _lru_layer(xs, state_lru_h[0], bufs0, *lru_params, nb=sb, tt=st, batch_major=False)
    ys, ss = _hgrn_decode(x1s, state_hgrn[0], *hgrn_params, nb=sb, tt=st, bstep=DECODE_SEQS_PER_STEP)
    bufs = bufs.reshape(halo_t, sb, W_LRU).transpose(1, 0, 2)

    return (y_prompt, ys.reshape(sb, st, D_MODEL), hp[None], bufp[None], sp[None],
            hs[None], bufs[None], ss[None])
```
